```python
import math
import jax, jax.numpy as jnp
from jax import lax
import numpy as np

D_MODEL = 2048
BATCH = 2
SEQ = 8192
DEPTH = 4

GRID_W = 64
CTX_LEN = 256
N_MIXERS = 2
N_ATTN = (DEPTH + N_MIXERS - 1) // N_MIXERS
N_CONV = DEPTH // N_MIXERS
DIFF_HEADS = 8
HEAD_DIM = 128
V_HEAD_DIM = 2 * HEAD_DIM
QK_WIDTH = DIFF_HEADS * 2 * HEAD_DIM
V_WIDTH = DIFF_HEADS * V_HEAD_DIM
QKV_WIDTH = 2 * QK_WIDTH + V_WIDTH
ROPE_THETA = 10000.0
Q_BLOCK = 128
CONV_W = 3
D_FF = 5632
N_MOD = 6
EPS = 1e-6

kernel_name = "hybrid_diffattn_shortconv_dit_trunk"


def rmsnorm(x, gain=None):
    xf = x.astype(jnp.float32)
    y = (xf * lax.rsqrt(jnp.mean(xf * xf, axis=-1, keepdims=True) + EPS)).astype(x.dtype)
    return y if gain is None else y * gain


def adaln(cond, w_ada, b_ada):
    mod = jax.nn.silu(cond) @ w_ada + b_ada
    return jnp.split(mod, N_MOD, axis=-1)


def modulate(h, shift, scale):
    return h * (1.0 + scale) + shift


def dwconv3(x, w):
    xp = jnp.pad(x, ((0, 0), (1, 1), (0, 0)))
    return xp[:, :-2] * w[0] + xp[:, 1:-1] * w[1] + xp[:, 2:] * w[2]


def axial_rope_tables(n_tokens):
    rows = n_tokens // GRID_W
    row = jnp.repeat(jnp.arange(rows, dtype=jnp.float32), GRID_W)
    col = jnp.tile(jnp.arange(GRID_W, dtype=jnp.float32), rows)
    half = HEAD_DIM // 2
    inv_freq = 1.0 / (ROPE_THETA ** (jnp.arange(0, half, 2, dtype=jnp.float32) / half))
    ang_r = row[:, None] * inv_freq
    ang_c = col[:, None] * inv_freq
    return (jnp.cos(ang_r), jnp.sin(ang_r), jnp.cos(ang_c), jnp.sin(ang_c))


def _rot_half(x, cos, sin):
    x1, x2 = jnp.split(x, 2, axis=-1)
    return jnp.concatenate([x1 * cos - x2 * sin, x2 * cos + x1 * sin], axis=-1)


def apply_axial_rope(x, tables):
    cr, sr, cc, sc = [t[None, :, None, None, :].astype(x.dtype) for t in tables]
    xr, xcol = jnp.split(x, 2, axis=-1)
    return jnp.concatenate([_rot_half(xr, cr, sr), _rot_half(xcol, cc, sc)], axis=-1)


def diff_attn_core(q, k, v, lam):
    s = jnp.einsum('bqhid,bkhid->bhiqk', q, k) * (HEAD_DIM ** -0.5)
    p = jax.nn.softmax(s.astype(jnp.float32), axis=-1)
    a = p[:, :, 0] - lam * p[:, :, 1]
    return jnp.einsum('bhqk,bkhe->bqhe', a.astype(v.dtype), v)


def latent_diff_attn(q, k_all, v_all, lam):
    b, t = q.shape[0], q.shape[1]
    nb = t // Q_BLOCK
    qb = q.reshape(b, nb, Q_BLOCK, DIFF_HEADS, 2, HEAD_DIM).transpose(1, 0, 2, 3, 4, 5)
    out = lax.map(lambda qblk: diff_attn_core(qblk, k_all, v_all, lam), qb)
    return out.transpose(1, 0, 2, 3, 4).reshape(b, t, DIFF_HEADS, V_HEAD_DIM)


def diff_head_out(o, subln_gain, lam_init, w_o):
    o = rmsnorm(o, subln_gain) * (1.0 - lam_init)
    return o.reshape(o.shape[0], o.shape[1], V_WIDTH) @ w_o


def split_qk(y, t):
    b = y.shape[0]
    return y.reshape(b, t, DIFF_HEADS, 2, HEAD_DIM)


def diff_attn_mixer(hl, hc, w_qkv, w_o, lq1, lk1, lq2, lk2, subln_gain, lam_init, rope_tables, need_ctx_out):
    b, t = hl.shape[0], hl.shape[1]
    lc = hc.shape[1]
    lam = (jnp.exp(jnp.sum(lq1.astype(jnp.float32) * lk1.astype(jnp.float32)))
           - jnp.exp(jnp.sum(lq2.astype(jnp.float32) * lk2.astype(jnp.float32))) + lam_init)
    yl = hl @ w_qkv
    ql = apply_axial_rope(split_qk(yl[..., :QK_WIDTH], t), rope_tables)
    kl = apply_axial_rope(split_qk(yl[..., QK_WIDTH:2 * QK_WIDTH], t), rope_tables)
    vl = yl[..., 2 * QK_WIDTH:].reshape(b, t, DIFF_HEADS, V_HEAD_DIM)
    if need_ctx_out:
        yc = hc @ w_qkv
        qc = split_qk(yc[..., :QK_WIDTH], lc)
        kv_c = yc[..., QK_WIDTH:]
    else:
        kv_c = hc @ w_qkv[:, QK_WIDTH:]
    kc = split_qk(kv_c[..., :QK_WIDTH], lc)
    vc = kv_c[..., QK_WIDTH:].reshape(b, lc, DIFF_HEADS, V_HEAD_DIM)
    k_all = jnp.concatenate([kl, kc], axis=1)
    v_all = jnp.concatenate([vl, vc], axis=1)
    out_l = diff_head_out(latent_diff_attn(ql, k_all, v_all, lam), subln_gain, lam_init, w_o)
    out_c = None
    if need_ctx_out:
        out_c = diff_head_out(diff_attn_core(qc, kc, vc, lam), subln_gain, lam_init, w_o)
    return out_l, out_c


def short_conv_mixer(h, w_in, conv_w, w_out):
    y = h @ w_in
    gb, gc, xv = jnp.split(y, 3, axis=-1)
    return (gb * dwconv3(gc * xv, conv_w)) @ w_out


def conv_ffn(h, w_up, conv_w, w_down):
    u = dwconv3(h @ w_up, conv_w)
    g, v = jnp.split(u, 2, axis=-1)
    return (jax.nn.silu(g) * v) @ w_down


def setup_inputs(seed: int = 0) -> dict:
    key = jax.random.key(seed)
    ks = jax.random.split(key, 24)
    f32 = jnp.float32
    nrm = lambda k, shape, s: jax.random.normal(k, shape, f32) * s
    d = D_MODEL
    return {
        "x": nrm(ks[0], (BATCH, SEQ, d), 1.0),
        "c": nrm(ks[1], (BATCH, d), 1.0),
        "ctx": nrm(ks[2], (BATCH, CTX_LEN, d), 1.0),
        "c_ctx": nrm(ks[3], (d,), 1.0),
        "w_ada": nrm(ks[4], (DEPTH, d, N_MOD * d), 0.5 * d ** -0.5),
        "b_ada": nrm(ks[5], (DEPTH, N_MOD * d), 0.02),
        "attn_w_qkv": nrm(ks[6], (N_ATTN, d, QKV_WIDTH), d ** -0.5),
        "attn_w_o": nrm(ks[7], (N_ATTN, V_WIDTH, d), V_WIDTH ** -0.5),
        "attn_lambda_q1": nrm(ks[8], (N_ATTN, HEAD_DIM), 0.1),
        "attn_lambda_k1": nrm(ks[9], (N_ATTN, HEAD_DIM), 0.1),
        "attn_lambda_q2": nrm(ks[10], (N_ATTN, HEAD_DIM), 0.1),
        "attn_lambda_k2": nrm(ks[11], (N_ATTN, HEAD_DIM), 0.1),
        "attn_subln_gain": 1.0 + nrm(ks[12], (N_ATTN, V_HEAD_DIM), 0.02),
        "sconv_w_in": nrm(ks[13], (N_CONV, d, 3 * d), d ** -0.5),
        "sconv_conv": nrm(ks[14], (N_CONV, CONV_W, d), CONV_W ** -0.5),
        "sconv_w_out": nrm(ks[15], (N_CONV, d, d), d ** -0.5),
        "ffn_w_up": nrm(ks[16], (DEPTH, d, 2 * D_FF), d ** -0.5),
        "ffn_conv": nrm(ks[17], (DEPTH, CONV_W, 2 * D_FF), CONV_W ** -0.5),
        "ffn_w_down": nrm(ks[18], (DEPTH, D_FF, d), D_FF ** -0.5),
        "final_gain": 1.0 + nrm(ks[19], (d,), 0.02),
    }


def reference(x, c, ctx, c_ctx, w_ada, b_ada, attn_w_qkv, attn_w_o, attn_lambda_q1, attn_lambda_k1,
              attn_lambda_q2, attn_lambda_k2, attn_subln_gain, sconv_w_in, sconv_conv, sconv_w_out,
              ffn_w_up, ffn_conv, ffn_w_down, final_gain):
    n_tokens = x.shape[1]
    rope_tables = axial_rope_tables(n_tokens)
    last_attn = max(i for i in range(DEPTH) if i % N_MIXERS == 0)
    xl, xc = x, ctx
    for i in range(DEPTH):
        ctx_update = i < last_attn
        ctx_read = i <= last_attn and i % N_MIXERS == 0
        sh1, sc1, g1, sh2, sc2, g2 = [m[:, None, :] for m in adaln(c, w_ada[i], b_ada[i])]
        hl = modulate(rmsnorm(xl), sh1, sc1)
        if ctx_update or ctx_read:
            csh1, csc1, cg1, csh2, csc2, cg2 = adaln(c_ctx, w_ada[i], b_ada[i])
            hc = modulate(rmsnorm(xc), csh1, csc1)
        if i % N_MIXERS == 0:
            a = i // N_MIXERS
            lam_init = 0.8 - 0.6 * math.exp(-0.3 * i)
            yl, yc = diff_attn_mixer(hl, hc, attn_w_qkv[a], attn_w_o[a], attn_lambda_q1[a], attn_lambda_k1[a],
                                     attn_lambda_q2[a], attn_lambda_k2[a], attn_subln_gain[a], lam_init,
                                     rope_tables, ctx_update)
        else:
            j = i // N_MIXERS
            yl = short_conv_mixer(hl, sconv_w_in[j], sconv_conv[j], sconv_w_out[j])
            if ctx_update:
                yc = short_conv_mixer(hc, sconv_w_in[j], sconv_conv[j], sconv_w_out[j])
        xl = xl + g1 * yl
        xl = xl + g2 * conv_ffn(modulate(rmsnorm(xl), sh2, sc2), ffn_w_up[i], ffn_conv[i], ffn_w_down[i])
        if ctx_update:
            xc = xc + cg1 * yc
            xc = xc + cg2 * conv_ffn(modulate(rmsnorm(xc), csh2, csc2), ffn_w_up[i], ffn_conv[i], ffn_w_down[i])
    return rmsnorm(xl, final_gain)
```

```python
import functools
import math

import jax
import jax.numpy as jnp
from jax import lax
from jax.experimental import pallas as pl
from jax.experimental.pallas import tpu as pltpu

F32 = jnp.float32
BF16 = jnp.bfloat16

GRID_W = 64
N_MIXERS = 2
DIFF_HEADS = 8
HEAD_DIM = 128
V_HEAD_DIM = 2 * HEAD_DIM
QK_WIDTH = DIFF_HEADS * 2 * HEAD_DIM
V_WIDTH = DIFF_HEADS * V_HEAD_DIM
ROPE_THETA = 10000.0
N_MOD = 6
EPS = 1e-6

LANES = 128
HALO = 16
V7X_VMEM_LIMIT = 56 * 1024 * 1024
ADA_ROWS = 8


def _pick(n, candidates):
    for c in candidates:
        if n % c == 0:
            return c
    return n


def _params(*sem):
    return pltpu.CompilerParams(dimension_semantics=sem, vmem_limit_bytes=V7X_VMEM_LIMIT)


def _norm_mod(x, shift, scale):
    ms = jnp.mean(x * x, axis=-1, keepdims=True)
    return (x * lax.rsqrt(ms + EPS)) * (1.0 + scale) + shift


def _ada_kernel(c_ref, w_ref, b_ref, o_ref):
    c = c_ref[...]
    s = (c * jax.nn.sigmoid(c)).astype(BF16)
    o_ref[0] = jnp.dot(s, w_ref[0].astype(BF16), preferred_element_type=F32) + b_ref[0]


def _ada_call(cond, w_ada, b_ada):
    depth, d, n = w_ada.shape
    tn = _pick(n, (1024, 512, 256, 128))
    return pl.pallas_call(
        _ada_kernel,
        grid=(depth, n // tn),
        in_specs=[
            pl.BlockSpec((ADA_ROWS, d), lambda l, j: (0, 0)),
            pl.BlockSpec((1, d, tn), lambda l, j: (l, 0, j)),
            pl.BlockSpec((1, 1, tn), lambda l, j: (l, 0, j)),
        ],
        out_specs=pl.BlockSpec((1, ADA_ROWS, tn), lambda l, j: (l, 0, j)),
        out_shape=jax.ShapeDtypeStruct((depth, ADA_ROWS, n), F32),
        compiler_params=_params("arbitrary", "arbitrary"),
        name="ada",
    )(cond, w_ada, b_ada.reshape(depth, 1, n))


def _rope(y, cos, sin_lo, sin_hi):
    return y * cos + pltpu.roll(y, LANES - 32, 1) * sin_lo + pltpu.roll(y, 32, 1) * sin_hi


def _qkv_kernel(x_ref, sh_ref, sc_ref, w_ref, cos_ref, slo_ref, shi_ref, o_ref, h_scr,
                *, col0, tn, rope):
    j = pl.program_id(1)

    @pl.when(j == 0)
    def _():
        h_scr[...] = _norm_mod(x_ref[...], sh_ref[0], sc_ref[0]).astype(BF16)

    y = jnp.dot(h_scr[...], w_ref[...], preferred_element_type=F32)
    col = (j + col0) * tn
    q_scale = HEAD_DIM ** -0.5

    def store(scale):
        for g in range(tn // LANES):
            yg = y[:, g * LANES:(g + 1) * LANES]
            if rope:
                yg = _rope(yg, cos_ref[...], slo_ref[...], shi_ref[...])
            if scale is not None:
                yg = yg * scale
            o_ref[:, g * LANES:(g + 1) * LANES] = yg.astype(o_ref.dtype)

    @pl.when(col < QK_WIDTH)
    def _():
        store(q_scale)

    @pl.when(jnp.logical_and(col >= QK_WIDTH, col < 2 * QK_WIDTH))
    def _():
        store(None)

    @pl.when(col >= 2 * QK_WIDTH)
    def _():
        o_ref[...] = y.astype(o_ref.dtype)


def _qkv_call(x, shift, scale, w, tables, *, tm, tiles_per_batch, col0_cols, rope):
    m, d = x.shape
    n = w.shape[1] - col0_cols
    tn = 512
    col0 = col0_cols // tn
    cos, slo, shi = tables
    pos_tiles = cos.shape[0] // tm
    tbl = pl.BlockSpec((tm, LANES), lambda i, j: (i % pos_tiles, 0))
    mod = pl.BlockSpec((1, 1, d), lambda i, j: (i // tiles_per_batch, 0, 0))
    return pl.pallas_call(
        functools.partial(_qkv_kernel, col0=col0, tn=tn, rope=rope),
        grid=(m // tm, n // tn),
        in_specs=[
            pl.BlockSpec((tm, d), lambda i, j: (i, 0)),
            mod, mod,
            pl.BlockSpec((d, tn), lambda i, j: (0, j + col0)),
            tbl, tbl, tbl,
        ],
        out_specs=pl.BlockSpec((tm, tn), lambda i, j: (i, j)),
        out_shape=jax.ShapeDtypeStruct((m, n), BF16),
        scratch_shapes=[pltpu.VMEM((tm, d), BF16)],
        compiler_params=_params("parallel", "arbitrary"),
        name="qkv",
    )(x, shift, scale, w, cos, slo, shi)


def _attn_kernel(lq1_ref, lk1_ref, lq2_ref, lk2_ref, gain_ref, q_ref, k_ref, v_ref, o_ref,
                 acc0, acc1, *, tk, n_chunks, lam_init):
    q = q_ref[...]
    q0 = q[:, :HEAD_DIM]
    q1 = q[:, HEAD_DIM:]
    tq = q.shape[0]
    acc0[...] = jnp.zeros_like(acc0)
    acc1[...] = jnp.zeros_like(acc1)
    nt = (((1,), (1,)), ((), ()))

    def one(qh, kh, v, acc, m, l):
        s = lax.dot_general(qh, kh, nt, preferred_element_type=F32)
        m_new = jnp.maximum(m, jnp.max(s, axis=-1, keepdims=True))
        alpha = jnp.exp(m - m_new)
        p = jnp.exp(s - m_new)
        l_new = alpha * l + jnp.sum(p, axis=-1, keepdims=True)
        acc[...] = alpha * acc[...] + jnp.dot(p.astype(BF16), v, preferred_element_type=F32)
        return m_new, l_new

    def body(c, carry):
        m0, l0, m1, l1 = carry
        start = pl.multiple_of(c * tk, tk)
        k = k_ref[pl.ds(start, tk), :]
        v = v_ref[pl.ds(start, tk), :]
        m0, l0 = one(q0, k[:, :HEAD_DIM], v, acc0, m0, l0)
        m1, l1 = one(q1, k[:, HEAD_DIM:], v, acc1, m1, l1)
        return m0, l0, m1, l1

    neg = jnp.full((tq, 1), -jnp.inf, F32)
    zero = jnp.zeros((tq, 1), F32)
    _, l0, _, l1 = lax.fori_loop(0, n_chunks, body, (neg, zero, neg, zero))

    lam = (jnp.exp(jnp.sum(lq1_ref[...] * lk1_ref[...], axis=-1, keepdims=True))
           - jnp.exp(jnp.sum(lq2_ref[...] * lk2_ref[...], axis=-1, keepdims=True)) + lam_init)
    o = acc0[...] / l0 - lam * (acc1[...] / l1)
    ms = jnp.mean(o * o, axis=-1, keepdims=True)
    o = (o * lax.rsqrt(ms + EPS)) * gain_ref[...] * (1.0 - lam_init)
    o_ref[...] = o.astype(o_ref.dtype)


def _attn_call(q_arr, kv_arr, lams, gain, *, batch, n_q, n_keys, lam_init):
    tq = _pick(n_q, (512, 256, 128))
    tk = _pick(n_keys, (768, 512, 256, 128))
    nq = n_q // tq
    hw = V_HEAD_DIM
    lam_spec = pl.BlockSpec((1, HEAD_DIM), lambda b, h, i: (0, 0))
    kv_spec = lambda off: pl.BlockSpec((None, n_keys, hw), lambda b, h, i: (b, 0, h + off))
    return pl.pallas_call(
        functools.partial(_attn_kernel, tk=tk, n_chunks=n_keys // tk, lam_init=lam_init),
        grid=(batch, DIFF_HEADS, nq),
        in_specs=[
            lam_spec, lam_spec, lam_spec, lam_spec,
            pl.BlockSpec((1, hw), lambda b, h, i: (0, 0)),
            pl.BlockSpec((tq, hw), lambda b, h, i: (b * nq + i, h)),
            kv_spec(0), kv_spec(DIFF_HEADS),
        ],
        out_specs=pl.BlockSpec((tq, hw), lambda b, h, i: (b * nq + i, h)),
        out_shape=jax.ShapeDtypeStruct((batch * n_q, V_WIDTH), BF16),
        scratch_shapes=[pltpu.VMEM((tq, hw), F32), pltpu.VMEM((tq, hw), F32)],
        compiler_params=_params("parallel", "parallel", "arbitrary"),
        name="diff_attn",
    )(*lams, gain, q_arr, kv_arr, kv_arr)


def _fill_h(h_scr, x_ref, xp_ref, xn_ref, sh, sc, first, last, tm):
    hp = _norm_mod(xp_ref[...], sh, sc)
    hn = _norm_mod(xn_ref[...], sh, sc)
    h_scr[0:HALO, :] = jnp.where(first, 0.0, hp).astype(BF16)
    h_scr[HALO:HALO + tm, :] = _norm_mod(x_ref[...], sh, sc).astype(BF16)
    h_scr[HALO + tm:, :] = jnp.where(last, 0.0, hn).astype(BF16)


def _conv3(y_scr, w_ref, tm):
    return (y_scr[pl.ds(HALO - 1, tm), :] * w_ref[0:1, :]
            + y_scr[pl.ds(HALO, tm), :] * w_ref[1:2, :]
            + y_scr[pl.ds(HALO + 1, tm), :] * w_ref[2:3, :])


def _halo_specs(tm, d, m):
    r = tm // HALO
    last_blk = m // HALO - 1
    return [
        pl.BlockSpec((tm, d), lambda i, j: (i, 0)),
        pl.BlockSpec((HALO, d), lambda i, j: (jnp.maximum(i * r - 1, 0), 0)),
        pl.BlockSpec((HALO, d), lambda i, j: (jnp.minimum((i + 1) * r, last_blk), 0)),
    ]


def _sconv_in_kernel(x_ref, xp_ref, xn_ref, sh_ref, sc_ref, wb_ref, wc_ref, wx_ref, cw_ref, o_ref,
                     h_scr, y_scr, *, tm, tiles_per_seq):
    i = pl.program_id(0)
    j = pl.program_id(1)

    @pl.when(j == 0)
    def _():
        t = i % tiles_per_seq
        _fill_h(h_scr, x_ref, xp_ref, xn_ref, sh_ref[0], sc_ref[0], t == 0, t == tiles_per_seq - 1, tm)

    h = h_scr[...]
    gc = jnp.dot(h, wc_ref[...], preferred_element_type=F32)
    xv = jnp.dot(h, wx_ref[...], preferred_element_type=F32)
    y_scr[...] = gc * xv
    gb = jnp.dot(h_scr[HALO:HALO + tm, :], wb_ref[...], preferred_element_type=F32)
    o_ref[...] = (gb * _conv3(y_scr, cw_ref, tm)).astype(o_ref.dtype)


def _sconv_in_call(x, shift, scale, w_in, conv_w, *, tm, tiles_per_batch, tiles_per_seq):
    m, d = x.shape
    n = w_in.shape[1] // 3
    tn = 512
    nt = n // tn
    mod = pl.BlockSpec((1, 1, d), lambda i, j: (i // tiles_per_batch, 0, 0))
    wspec = lambda off: pl.BlockSpec((d, tn), lambda i, j: (0, j + off))
    return pl.pallas_call(
        functools.partial(_sconv_in_kernel, tm=tm, tiles_per_seq=tiles_per_seq),
        grid=(m // tm, nt),
        in_specs=_halo_specs(tm, d, m) + [
            mod, mod, wspec(0), wspec(nt), wspec(2 * nt),
            pl.BlockSpec((3, tn), lambda i, j: (0, j)),
        ],
        out_specs=pl.BlockSpec((tm, tn), lambda i, j: (i, j)),
        out_shape=jax.ShapeDtypeStruct((m, n), BF16),
        scratch_shapes=[pltpu.VMEM((tm + 2 * HALO, d), BF16), pltpu.VMEM((tm + 2 * HALO, tn), F32)],
        compiler_params=_params("parallel", "arbitrary"),
        name="sconv_in",
    )(x, x, x, shift, scale, w_in, w_in, w_in, conv_w)


def _ffn_up_kernel(x_ref, xp_ref, xn_ref, sh_ref, sc_ref, wg_ref, wv_ref, cg_ref, cv_ref, o_ref,
                   h_scr, yg_scr, yv_scr, *, tm, tiles_per_seq):
    i = pl.program_id(0)
    j = pl.program_id(1)

    @pl.when(j == 0)
    def _():
        t = i % tiles_per_seq
        _fill_h(h_scr, x_ref, xp_ref, xn_ref, sh_ref[0], sc_ref[0], t == 0, t == tiles_per_seq - 1, tm)

    h = h_scr[...]
    yg_scr[...] = jnp.dot(h, wg_ref[...], preferred_element_type=F32)
    yv_scr[...] = jnp.dot(h, wv_ref[...], preferred_element_type=F32)
    ug = _conv3(yg_scr, cg_ref, tm)
    uv = _conv3(yv_scr, cv_ref, tm)
    o_ref[...] = (ug * jax.nn.sigmoid(ug) * uv).astype(o_ref.dtype)


def _ffn_up_call(x, shift, scale, w_up, conv_w, *, tm, tiles_per_batch, tiles_per_seq):
    m, d = x.shape
    n = w_up.shape[1] // 2
    tn = 512
    nt = n // tn
    mod = pl.BlockSpec((1, 1, d), lambda i, j: (i // tiles_per_batch, 0, 0))
    wspec = lambda off: pl.BlockSpec((d, tn), lambda i, j: (0, j + off))
    cspec = lambda off: pl.BlockSpec((3, tn), lambda i, j: (0, j + off))
    ext = tm + 2 * HALO
    return pl.pallas_call(
        functools.partial(_ffn_up_kernel, tm=tm, tiles_per_seq=tiles_per_seq),
        grid=(m // tm, nt),
        in_specs=_halo_specs(tm, d, m) + [mod, mod, wspec(0), wspec(nt), cspec(0), cspec(nt)],
        out_specs=pl.BlockSpec((tm, tn), lambda i, j: (i, j)),
        out_shape=jax.ShapeDtypeStruct((m, n), BF16),
        scratch_shapes=[pltpu.VMEM((ext, d), BF16), pltpu.VMEM((ext, tn), F32), pltpu.VMEM((ext, tn), F32)],
        compiler_params=_params("parallel", "arbitrary"),
        name="ffn_up",
    )(x, x, x, shift, scale, w_up, w_up, conv_w, conv_w)


def _proj_res_kernel(a_ref, w_ref, res_ref, g_ref, o_ref):
    y = jnp.dot(a_ref[...], w_ref[...], preferred_element_type=F32)
    o_ref[...] = res_ref[...] + g_ref[0] * y


def _proj_res_call(a, w, res, gate, *, tm, tiles_per_batch):
    m, k = a.shape
    n = w.shape[1]
    tn = n if k * n * 2 <= 8 * 1024 * 1024 else 512
    return pl.pallas_call(
        _proj_res_kernel,
        grid=(m // tm, n // tn),
        in_specs=[
            pl.BlockSpec((tm, k), lambda i, j: (i, 0)),
            pl.BlockSpec((k, tn), lambda i, j: (0, j)),
            pl.BlockSpec((tm, tn), lambda i, j: (i, j)),
            pl.BlockSpec((1, 1, tn), lambda i, j: (i // tiles_per_batch, 0, j)),
        ],
        out_specs=pl.BlockSpec((tm, tn), lambda i, j: (i, j)),
        out_shape=jax.ShapeDtypeStruct((m, n), F32),
        compiler_params=_params("parallel", "arbitrary"),
        name="proj_res",
    )(a, w, res, gate)


def _final_norm_kernel(x_ref, g_ref, o_ref):
    x = x_ref[...]
    ms = jnp.mean(x * x, axis=-1, keepdims=True)
    o_ref[...] = (x * lax.rsqrt(ms + EPS)) * g_ref[...]


def _final_norm_call(x, gain, *, tm):
    m, d = x.shape
    return pl.pallas_call(
        _final_norm_kernel,
        grid=(m // tm,),
        in_specs=[pl.BlockSpec((tm, d), lambda i: (i, 0)), pl.BlockSpec((1, d), lambda i: (0, 0))],
        out_specs=pl.BlockSpec((tm, d), lambda i: (i, 0)),
        out_shape=jax.ShapeDtypeStruct((m, d), F32),
        compiler_params=_params("parallel"),
        name="final_norm",
    )(x, gain.reshape(1, d))


def _rope_tables(n_tokens):
    rows = n_tokens // GRID_W
    row = jnp.repeat(jnp.arange(rows, dtype=F32), GRID_W)
    col = jnp.tile(jnp.arange(GRID_W, dtype=F32), rows)
    half = HEAD_DIM // 2
    inv_freq = 1.0 / (ROPE_THETA ** (jnp.arange(0, half, 2, dtype=F32) / half))
    ang_r = row[:, None] * inv_freq
    ang_c = col[:, None] * inv_freq
    zeros = jnp.zeros_like(ang_r)
    cos = jnp.concatenate([jnp.cos(ang_r)] * 2 + [jnp.cos(ang_c)] * 2, axis=-1)
    sin_lo = jnp.concatenate([-jnp.sin(ang_r), zeros, -jnp.sin(ang_c), zeros], axis=-1)
    sin_hi = jnp.concatenate([zeros, jnp.sin(ang_r), zeros, jnp.sin(ang_c)], axis=-1)
    return cos, sin_lo, sin_hi


def kernel(x, c, ctx, c_ctx, w_ada, b_ada, attn_w_qkv, attn_w_o, attn_lambda_q1, attn_lambda_k1,
           attn_lambda_q2, attn_lambda_k2, attn_subln_gain, sconv_w_in, sconv_conv, sconv_w_out,
           ffn_w_up, ffn_conv, ffn_w_down, final_gain):
    b, t, d = x.shape
    lc = ctx.shape[1]
    depth = w_ada.shape[0]
    assert b + 1 <= ADA_ROWS and t % GRID_W == 0
    tm_l = _pick(t, (1024, 512, 256, 128))
    tm_c = _pick(lc, (256, 128))
    lat = dict(tm=tm_l, tiles_per_batch=t // tm_l)
    cx = dict(tm=tm_c, tiles_per_batch=(b * lc) // tm_c)
    lat_seq = dict(tiles_per_seq=t // tm_l, **lat)
    cx_seq = dict(tiles_per_seq=lc // tm_c, **cx)

    cond = jnp.zeros((ADA_ROWS, d), F32).at[:b].set(c).at[b].set(c_ctx)
    mods = _ada_call(cond, w_ada, b_ada).reshape(depth, ADA_ROWS, N_MOD, d)
    tables = _rope_tables(t)
    last_attn = max(i for i in range(depth) if i % N_MIXERS == 0)

    xl = x.reshape(b * t, d)
    xc = ctx.reshape(b * lc, d)
    for i in range(depth):
        ctx_update = i < last_attn
        is_attn = i % N_MIXERS == 0
        ctx_read = i <= last_attn and is_attn
        ml = [mods[i, :b, k].reshape(b, 1, d) for k in range(N_MOD)]
        mc = [mods[i, b:b + 1, k].reshape(1, 1, d) for k in range(N_MOD)]
        w_up = ffn_w_up[i].astype(BF16)
        w_down = ffn_w_down[i].astype(BF16)
        if is_attn:
            a = i // N_MIXERS
            lam_init = 0.8 - 0.6 * math.exp(-0.3 * i)
            w_qkv = attn_w_qkv[a].astype(BF16)
            w_o = attn_w_o[a].astype(BF16)
            lams = [p[a].reshape(1, HEAD_DIM) for p in
                    (attn_lambda_q1, attn_lambda_k1, attn_lambda_q2, attn_lambda_k2)]
            gain = attn_subln_gain[a].reshape(1, V_HEAD_DIM)
            qkv_l = _qkv_call(xl, ml[0], ml[1], w_qkv, tables, col0_cols=0, rope=True, **lat)
            kv_l = qkv_l[:, QK_WIDTH:].reshape(b, t, QK_WIDTH + V_WIDTH)
            if ctx_update:
                qkv_c = _qkv_call(xc, mc[0], mc[1], w_qkv, tables, col0_cols=0, rope=False, **cx)
                kv_c = qkv_c[:, QK_WIDTH:]
            else:
                kv_c = _qkv_call(xc, mc[0], mc[1], w_qkv, tables, col0_cols=QK_WIDTH, rope=False, **cx)
            kv_c = kv_c.reshape(b, lc, QK_WIDTH + V_WIDTH)
            kv_all = jnp.concatenate([kv_l, kv_c], axis=1)
            o_l = _attn_call(qkv_l, kv_all, lams, gain, batch=b, n_q=t, n_keys=t + lc, lam_init=lam_init)
            xl = _proj_res_call(o_l, w_o, xl, ml[2], **lat)
            if ctx_update:
                o_c = _attn_call(qkv_c, kv_c, lams, gain, batch=b, n_q=lc, n_keys=lc, lam_init=lam_init)
                xc = _proj_res_call(o_c, w_o, xc, mc[2], **cx)
        else:
            jx = i // N_MIXERS
            w_in = sconv_w_in[jx].astype(BF16)
            w_out = sconv_w_out[jx].astype(BF16)
            z_l = _sconv_in_call(xl, ml[0], ml[1], w_in, sconv_conv[jx], **lat_seq)
            xl = _proj_res_call(z_l, w_out, xl, ml[2], **lat)
            if ctx_update:
                z_c = _sconv_in_call(xc, mc[0], mc[1], w_in, sconv_conv[jx], **cx_seq)
                xc = _proj_res_call(z_c, w_out, xc, mc[2], **cx)
        u_l = _ffn_up_call(xl, ml[3], ml[4], w_up, ffn_conv[i], **lat_seq)
        xl = _proj_res_call(u_l, w_down, xl, ml[5], **lat)
        if ctx_update:
            u_c = _ffn_up_call(xc, mc[3], mc[4], w_up, ffn_conv[i], **cx_seq)
            xc = _proj_res_call(u_c, w_down, xc, mc[5], **cx)
    return _final_norm_call(xl, final_gain, tm=tm_l).reshape(b, t, d)
```

```python
import functools
import math

import jax
import jax.numpy as jnp
from jax import lax
from jax.experimental import pallas as pl
from jax.experimental.pallas import tpu as pltpu

F32 = jnp.float32
BF16 = jnp.bfloat16

GRID_W = 64
N_MIXERS = 2
DIFF_HEADS = 8
HEAD_DIM = 128
V_HEAD_DIM = 2 * HEAD_DIM
QK_WIDTH = DIFF_HEADS * 2 * HEAD_DIM
V_WIDTH = DIFF_HEADS * V_HEAD_DIM
ROPE_THETA = 10000.0
N_MOD = 6
EPS = 1e-6

LANES = 128
HALO = 16
V7X_VMEM_LIMIT = 56 * 1024 * 1024
ADA_ROWS = 8
TN = 512
Q_SCALE = HEAD_DIM ** -0.5 * math.log2(math.e)


def _pick(n, candidates):
    for c in candidates:
        if n % c == 0:
            return c
    return n


def _params(*sem):
    return pltpu.CompilerParams(dimension_semantics=sem, vmem_limit_bytes=V7X_VMEM_LIMIT)


def _norm_mod(x, shift, scale):
    ms = jnp.mean(x * x, axis=-1, keepdims=True)
    return (x * lax.rsqrt(ms + EPS)) * (1.0 + scale) + shift


def _col_tiles(w, tn):
    k, n = w.shape
    return w.astype(BF16).reshape(k, n // tn, tn).swapaxes(0, 1)


def _ada_kernel(c_ref, w_ref, b_ref, o_ref):
    c = c_ref[...]
    s = (c * jax.nn.sigmoid(c)).astype(BF16)
    o_ref[0] = jnp.dot(s, w_ref[0].astype(BF16), preferred_element_type=F32) + b_ref[0]


def _ada_call(cond, w_ada, b_ada):
    depth, d, n = w_ada.shape
    tn = _pick(n, (1024, 512, 256, 128))
    return pl.pallas_call(
        _ada_kernel,
        grid=(depth, n // tn),
        in_specs=[
            pl.BlockSpec((ADA_ROWS, d), lambda l, j: (0, 0)),
            pl.BlockSpec((1, d, tn), lambda l, j: (l, 0, j)),
            pl.BlockSpec((1, 1, tn), lambda l, j: (l, 0, j)),
        ],
        out_specs=pl.BlockSpec((1, ADA_ROWS, tn), lambda l, j: (l, 0, j)),
        out_shape=jax.ShapeDtypeStruct((depth, ADA_ROWS, n), F32),
        compiler_params=_params("arbitrary", "arbitrary"),
        name="ada",
    )(cond, w_ada, b_ada.reshape(depth, 1, n))


def _rope(y, cos, sin_lo, sin_hi):
    return y * cos + pltpu.roll(y, LANES - 32, 1) * sin_lo + pltpu.roll(y, 32, 1) * sin_hi


def _qkv_kernel(x_ref, sh_ref, sc_ref, w_ref, cos_ref, slo_ref, shi_ref, *rest, tn, n_q, n_k, rope):
    if n_q:
        qt_ref, k_ref, vt_ref, h_scr = rest
    else:
        k_ref, vt_ref, h_scr = rest
    j = pl.program_id(1)

    @pl.when(j == 0)
    def _():
        h_scr[...] = _norm_mod(x_ref[...], sh_ref[0], sc_ref[0]).astype(BF16)

    y = jnp.dot(h_scr[...], w_ref[...], preferred_element_type=F32)

    def groups():
        for g in range(tn // LANES):
            yg = y[:, g * LANES:(g + 1) * LANES]
            if rope:
                yg = _rope(yg, cos_ref[...], slo_ref[...], shi_ref[...])
            yield g, yg

    if n_q:
        @pl.when(j < n_q)
        def _():
            for g, yg in groups():
                qt_ref[g * LANES:(g + 1) * LANES, :] = (yg * Q_SCALE).T.astype(BF16)

    @pl.when(jnp.logical_and(j >= n_q, j < n_q + n_k))
    def _():
        for g, yg in groups():
            k_ref[:, g * LANES:(g + 1) * LANES] = yg.astype(BF16)

    @pl.when(j >= n_q + n_k)
    def _():
        vt_ref[...] = y.T.astype(BF16)


def _qkv_call(x, shift, scale, w_tiles, tables, *, n_seq, tm, tiles_per_mod, with_q, rope):
    m, d = x.shape
    tn = w_tiles.shape[2]
    seq = m // n_seq
    tps = seq // tm
    n_q = QK_WIDTH // tn if with_q else 0
    n_k = QK_WIDTH // tn
    n_v = V_WIDTH // tn
    col0 = 0 if with_q else QK_WIDTH // tn
    cos, slo, shi = tables
    pos_tiles = max(cos.shape[0] // tm, 1)
    tbl = pl.BlockSpec((tm, LANES), lambda i, j: (i % pos_tiles, 0))
    mod = pl.BlockSpec((1, 1, d), lambda i, j: (i // tiles_per_mod, 0, 0))
    out_specs = [
        pl.BlockSpec((tm, tn), lambda i, j: (i, jnp.clip(j - n_q, 0, n_k - 1))),
        pl.BlockSpec((None, None, tn, tm),
                     lambda i, j: (i // tps, i % tps, jnp.clip(j - n_q - n_k, 0, n_v - 1), 0)),
    ]
    out_shape = [
        jax.ShapeDtypeStruct((m, QK_WIDTH), BF16),
        jax.ShapeDtypeStruct((n_seq, tps, V_WIDTH, tm), BF16),
    ]
    if with_q:
        out_specs.insert(0, pl.BlockSpec((None, tn, tm), lambda i, j: (i // tps, jnp.minimum(j, n_q - 1), i % tps)))
        out_shape.insert(0, jax.ShapeDtypeStruct((n_seq, QK_WIDTH, seq), BF16))
    return pl.pallas_call(
        functools.partial(_qkv_kernel, tn=tn, n_q=n_q, n_k=n_k, rope=rope),
        grid=(m // tm, n_q + n_k + n_v),
        in_specs=[
            pl.BlockSpec((tm, d), lambda i, j: (i, 0)),
            mod, mod,
            pl.BlockSpec((None, d, tn), lambda i, j: (j + col0, 0, 0)),
            tbl, tbl, tbl,
        ],
        out_specs=out_specs,
        out_shape=out_shape,
        scratch_shapes=[pltpu.VMEM((tm, d), BF16)],
        compiler_params=_params("parallel", "arbitrary"),
        name="qkv",
    )(x, shift, scale, w_tiles, cos, slo, shi)


def _attn_kernel(lq1_ref, lk1_ref, lq2_ref, lk2_ref, gain_ref, qt_ref, *rest, chunks, lam_init):
    n_src = len(chunks)
    srcs = [(rest[2 * s], rest[2 * s + 1]) for s in range(n_src)]
    o_ref, acc0, acc1, sa0, sa1, sb0, sb1 = rest[2 * n_src:]
    qt = qt_ref[...]
    q0t = qt[:HEAD_DIM, :]
    q1t = qt[HEAD_DIM:, :]
    tq = qt.shape[1]
    acc0[...] = jnp.zeros_like(acc0)
    acc1[...] = jnp.zeros_like(acc1)

    def scores(k):
        return (jnp.dot(k[:, :HEAD_DIM], q0t, preferred_element_type=F32),
                jnp.dot(k[:, HEAD_DIM:], q1t, preferred_element_type=F32))

    def update(s, mc, vt, acc, m, l):
        m_new = jnp.maximum(m, mc)
        alpha = jnp.exp2(m - m_new)
        p = jnp.exp2(s - m_new)
        l_new = alpha * l + jnp.sum(p, axis=0, keepdims=True)
        acc[...] = alpha * acc[...] + jnp.dot(vt, p.astype(BF16), preferred_element_type=F32)
        return m_new, l_new

    def colmax(s):
        return jnp.max(s, axis=0, keepdims=True)

    def step(k, vt, carry):
        m0, l0, m1, l1 = carry
        s0, s1 = scores(k)
        m0, l0 = update(s0, colmax(s0), vt, acc0, m0, l0)
        m1, l1 = update(s1, colmax(s1), vt, acc1, m1, l1)
        return m0, l0, m1, l1

    def stage(k, s0_scr, s1_scr):
        s0, s1 = scores(k)
        s0_scr[...] = s0
        s1_scr[...] = s1
        return colmax(s0), colmax(s1)

    def consume(s0_scr, s1_scr, mc, vt, carry):
        m0, l0, m1, l1 = carry
        m0, l0 = update(s0_scr[...], mc[0], vt, acc0, m0, l0)
        m1, l1 = update(s1_scr[...], mc[1], vt, acc1, m1, l1)
        return m0, l0, m1, l1

    neg = jnp.full((1, tq), -jnp.inf, F32)
    zero = jnp.zeros((1, tq), F32)
    carry = (neg, zero, neg, zero)
    for (k_ref, vt_ref), (n_chunks, tk) in zip(srcs, chunks):
        chunk = lambda c, k_ref=k_ref, tk=tk: k_ref[pl.ds(pl.multiple_of(c * tk, tk), tk), :]
        if n_chunks == 1:
            carry = step(k_ref[...], vt_ref[0], carry)
        elif n_chunks % 2 == 0 and sa0.shape[0] == tk:
            def body(i, state, chunk=chunk, vt_ref=vt_ref):
                carry, mca = state[:4], state[4:]
                c = 2 * i
                mcb = stage(chunk(c + 1), sb0, sb1)
                carry = consume(sa0, sa1, mca, vt_ref[c], carry)
                mca = stage(chunk(c + 2), sa0, sa1)
                carry = consume(sb0, sb1, mcb, vt_ref[c + 1], carry)
                return carry + mca
            mca = stage(chunk(0), sa0, sa1)
            state = lax.fori_loop(0, n_chunks // 2 - 1, body, carry + mca)
            carry, mca = state[:4], state[4:]
            mcb = stage(chunk(n_chunks - 1), sb0, sb1)
            carry = consume(sa0, sa1, mca, vt_ref[n_chunks - 2], carry)
            carry = consume(sb0, sb1, mcb, vt_ref[n_chunks - 1], carry)
        else:
            carry = lax.fori_loop(
                0, n_chunks, lambda c, carry, chunk=chunk, vt_ref=vt_ref: step(chunk(c), vt_ref[c], carry), carry)
    _, l0, _, l1 = carry

    lam = (jnp.exp(jnp.sum(lq1_ref[...] * lk1_ref[...], axis=-1, keepdims=True))
           - jnp.exp(jnp.sum(lq2_ref[...] * lk2_ref[...], axis=-1, keepdims=True)) + lam_init)
    ot = acc0[...] / l0 - lam * (acc1[...] / l1)
    ms = jnp.mean(ot * ot, axis=0, keepdims=True)
    o = (ot * lax.rsqrt(ms + EPS)).T * gain_ref[...] * (1.0 - lam_init)
    o_ref[...] = o.astype(o_ref.dtype)


def _attn_call(qt, srcs, lams, gain, *, lam_init):
    n_seq, _, n_q = qt.shape
    tq = _pick(n_q, (512, 256, 128))
    nq = n_q // tq
    hw = V_HEAD_DIM
    lam_spec = pl.BlockSpec((1, HEAD_DIM), lambda b, h, i: (0, 0))
    in_specs = [
        lam_spec, lam_spec, lam_spec, lam_spec,
        pl.BlockSpec((1, hw), lambda b, h, i: (0, 0)),
        pl.BlockSpec((None, hw, tq), lambda b, h, i: (b, h, i)),
    ]
    args, chunks = [], []
    for k, vt in srcs:
        n_chunks, tk = vt.shape[1], vt.shape[3]
        chunks.append((n_chunks, tk))
        in_specs.append(pl.BlockSpec((None, k.shape[1], hw), lambda b, h, i: (b, 0, h)))
        in_specs.append(pl.BlockSpec((None, n_chunks, hw, tk), lambda b, h, i: (b, 0, h, 0)))
        args += [k, vt]
    stage_tk = max(chunks)[1] if max(chunks)[0] > 1 else 8
    s_scr = pltpu.VMEM((stage_tk, tq), F32)
    return pl.pallas_call(
        functools.partial(_attn_kernel, chunks=tuple(chunks), lam_init=lam_init),
        grid=(n_seq, DIFF_HEADS, nq),
        in_specs=in_specs,
        out_specs=pl.BlockSpec((tq, hw), lambda b, h, i: (b * nq + i, h)),
        out_shape=jax.ShapeDtypeStruct((n_seq * n_q, V_WIDTH), BF16),
        scratch_shapes=[pltpu.VMEM((hw, tq), F32), pltpu.VMEM((hw, tq), F32), s_scr, s_scr, s_scr, s_scr],
        compiler_params=_params("parallel", "parallel", "arbitrary"),
        name="diff_attn",
    )(*lams, gain, qt, *args)


def _fill_h(h_scr, x_ref, xp_ref, xn_ref, sh, sc, first, last, tm):
    hp = _norm_mod(xp_ref[...], sh, sc)
    hn = _norm_mod(xn_ref[...], sh, sc)
    h_scr[0:HALO, :] = jnp.where(first, 0.0, hp).astype(BF16)
    h_scr[HALO:HALO + tm, :] = _norm_mod(x_ref[...], sh, sc).astype(BF16)
    h_scr[HALO + tm:, :] = jnp.where(last, 0.0, hn).astype(BF16)


def _conv3(y_scr, w_ref, tm):
    return (y_scr[pl.ds(HALO - 1, tm), :] * w_ref[0:1, :]
            + y_scr[pl.ds(HALO, tm), :] * w_ref[1:2, :]
            + y_scr[pl.ds(HALO + 1, tm), :] * w_ref[2:3, :])


def _halo_specs(tm, d, m):
    r = tm // HALO
    last_blk = m // HALO - 1
    return [
        pl.BlockSpec((tm, d), lambda i, j: (i, 0)),
        pl.BlockSpec((HALO, d), lambda i, j: (jnp.maximum(i * r - 1, 0), 0)),
        pl.BlockSpec((HALO, d), lambda i, j: (jnp.minimum((i + 1) * r, last_blk), 0)),
    ]


def _sconv_in_kernel(x_ref, xp_ref, xn_ref, sh_ref, sc_ref, wb_ref, wc_ref, wx_ref, cw_ref, o_ref,
                     h_scr, y_scr, *, tm, tiles_per_seq):
    i = pl.program_id(0)
    j = pl.program_id(1)

    @pl.when(j == 0)
    def _():
        t = i % tiles_per_seq
        _fill_h(h_scr, x_ref, xp_ref, xn_ref, sh_ref[0], sc_ref[0], t == 0, t == tiles_per_seq - 1, tm)

    h = h_scr[...]
    gc = jnp.dot(h, wc_ref[...], preferred_element_type=F32)
    xv = jnp.dot(h, wx_ref[...], preferred_element_type=F32)
    y_scr[...] = gc * xv
    gb = jnp.dot(h_scr[HALO:HALO + tm, :], wb_ref[...], preferred_element_type=F32)
    o_ref[...] = (gb * _conv3(y_scr, cw_ref, tm)).astype(o_ref.dtype)


def _sconv_in_call(x, shift, scale, w_tiles, conv_w, *, tm, tiles_per_mod, tiles_per_seq):
    m, d = x.shape
    tn = w_tiles.shape[2]
    nt = w_tiles.shape[0] // 3
    mod = pl.BlockSpec((1, 1, d), lambda i, j: (i // tiles_per_mod, 0, 0))
    wspec = lambda off: pl.BlockSpec((None, d, tn), lambda i, j: (j + off, 0, 0))
    return pl.pallas_call(
        functools.partial(_sconv_in_kernel, tm=tm, tiles_per_seq=tiles_per_seq),
        grid=(m // tm, nt),
        in_specs=_halo_specs(tm, d, m) + [
            mod, mod, wspec(0), wspec(nt), wspec(2 * nt),
            pl.BlockSpec((3, tn), lambda i, j: (0, j)),
        ],
        out_specs=pl.BlockSpec((tm, tn), lambda i, j: (i, j)),
        out_shape=jax.ShapeDtypeStruct((m, nt * tn), BF16),
        scratch_shapes=[pltpu.VMEM((tm + 2 * HALO, d), BF16), pltpu.VMEM((tm + 2 * HALO, tn), F32)],
        compiler_params=_params("parallel", "arbitrary"),
        name="sconv_in",
    )(x, x, x, shift, scale, w_tiles, w_tiles, w_tiles, conv_w)


def _ffn_up_kernel(x_ref, xp_ref, xn_ref, sh_ref, sc_ref, wg_ref, wv_ref, cg_ref, cv_ref, o_ref,
                   h_scr, yg_scr, yv_scr, *, tm, tiles_per_seq):
    i = pl.program_id(0)
    j = pl.program_id(1)

    @pl.when(j == 0)
    def _():
        t = i % tiles_per_seq
        _fill_h(h_scr, x_ref, xp_ref, xn_ref, sh_ref[0], sc_ref[0], t == 0, t == tiles_per_seq - 1, tm)

    h = h_scr[...]
    yg_scr[...] = jnp.dot(h, wg_ref[...], preferred_element_type=F32)
    yv_scr[...] = jnp.dot(h, wv_ref[...], preferred_element_type=F32)
    ug = _conv3(yg_scr, cg_ref, tm)
    uv = _conv3(yv_scr, cv_ref, tm)
    o_ref[...] = (ug * jax.nn.sigmoid(ug) * uv).astype(o_ref.dtype)


def _ffn_up_call(x, shift, scale, w_tiles, conv_w, *, tm, tiles_per_mod, tiles_per_seq):
    m, d = x.shape
    tn = w_tiles.shape[2]
    nt = w_tiles.shape[0] // 2
    mod = pl.BlockSpec((1, 1, d), lambda i, j: (i // tiles_per_mod, 0, 0))
    wspec = lambda off: pl.BlockSpec((None, d, tn), lambda i, j: (j + off, 0, 0))
    cspec = lambda off: pl.BlockSpec((3, tn), lambda i, j: (0, j + off))
    ext = tm + 2 * HALO
    return pl.pallas_call(
        functools.partial(_ffn_up_kernel, tm=tm, tiles_per_seq=tiles_per_seq),
        grid=(m // tm, nt),
        in_specs=_halo_specs(tm, d, m) + [mod, mod, wspec(0), wspec(nt), cspec(0), cspec(nt)],
        out_specs=pl.BlockSpec((tm, tn), lambda i, j: (i, j)),
        out_shape=jax.ShapeDtypeStruct((m, nt * tn), BF16),
        scratch_shapes=[pltpu.VMEM((ext, d), BF16), pltpu.VMEM((ext, tn), F32), pltpu.VMEM((ext, tn), F32)],
        compiler_params=_params("parallel", "arbitrary"),
        name="ffn_up",
    )(x, x, x, shift, scale, w_tiles, w_tiles, conv_w, conv_w)


def _proj_res_kernel(a_ref, w_ref, res_ref, g_ref, o_ref):
    y = jnp.dot(a_ref[...], w_ref[...], preferred_element_type=F32)
    o_ref[...] = res_ref[...] + g_ref[0] * y


def _proj_res_call(a, w_tiles, res, gate, *, tm, tiles_per_mod):
    m, k = a.shape
    nt, _, tn = w_tiles.shape
    return pl.pallas_call(
        _proj_res_kernel,
        grid=(m // tm, nt),
        in_specs=[
            pl.BlockSpec((tm, k), lambda i, j: (i, 0)),
            pl.BlockSpec((None, k, tn), lambda i, j: (j, 0, 0)),
            pl.BlockSpec((tm, tn), lambda i, j: (i, j)),
            pl.BlockSpec((1, 1, tn), lambda i, j: (i // tiles_per_mod, 0, j)),
        ],
        out_specs=pl.BlockSpec((tm, tn), lambda i, j: (i, j)),
        out_shape=jax.ShapeDtypeStruct((m, nt * tn), F32),
        compiler_params=_params("parallel", "arbitrary"),
        name="proj_res",
    )(a, w_tiles, res, gate)


def _final_norm_kernel(x_ref, g_ref, o_ref):
    x = x_ref[...]
    ms = jnp.mean(x * x, axis=-1, keepdims=True)
    o_ref[...] = (x * lax.rsqrt(ms + EPS)) * g_ref[...]


def _final_norm_call(x, gain, *, tm):
    m, d = x.shape
    return pl.pallas_call(
        _final_norm_kernel,
        grid=(m // tm,),
        in_specs=[pl.BlockSpec((tm, d), lambda i: (i, 0)), pl.BlockSpec((1, d), lambda i: (0, 0))],
        out_specs=pl.BlockSpec((tm, d), lambda i: (i, 0)),
        out_shape=jax.ShapeDtypeStruct((m, d), F32),
        compiler_params=_params("parallel"),
        name="final_norm",
    )(x, gain.reshape(1, d))


def _rope_tables(n_tokens):
    rows = n_tokens // GRID_W
    row = jnp.repeat(jnp.arange(rows, dtype=F32), GRID_W)
    col = jnp.tile(jnp.arange(GRID_W, dtype=F32), rows)
    half = HEAD_DIM // 2
    inv_freq = 1.0 / (ROPE_THETA ** (jnp.arange(0, half, 2, dtype=F32) / half))
    ang_r = row[:, None] * inv_freq
    ang_c = col[:, None] * inv_freq
    zeros = jnp.zeros_like(ang_r)
    cos = jnp.concatenate([jnp.cos(ang_r)] * 2 + [jnp.cos(ang_c)] * 2, axis=-1)
    sin_lo = jnp.concatenate([-jnp.sin(ang_r), zeros, -jnp.sin(ang_c), zeros], axis=-1)
    sin_hi = jnp.concatenate([zeros, jnp.sin(ang_r), zeros, jnp.sin(ang_c)], axis=-1)
    return cos, sin_lo, sin_hi


def kernel(x, c, ctx, c_ctx, w_ada, b_ada, attn_w_qkv, attn_w_o, attn_lambda_q1, attn_lambda_k1,
           attn_lambda_q2, attn_lambda_k2, attn_subln_gain, sconv_w_in, sconv_conv, sconv_w_out,
           ffn_w_up, ffn_conv, ffn_w_down, final_gain):
    b, t, d = x.shape
    lc = ctx.shape[1]
    depth = w_ada.shape[0]
    assert b + 1 <= ADA_ROWS and t % GRID_W == 0
    tm_l = _pick(t, (1024, 512, 256, 128))
    tm_c = _pick(lc, (256, 128))
    lat = dict(tm=tm_l, tiles_per_mod=t // tm_l)
    cx = dict(tm=tm_c, tiles_per_mod=(b * lc) // tm_c)
    lat_seq = dict(tiles_per_seq=t // tm_l, **lat)
    cx_seq = dict(tiles_per_seq=lc // tm_c, **cx)

    cond = jnp.zeros((ADA_ROWS, d), F32).at[:b].set(c).at[b].set(c_ctx)
    mods = _ada_call(cond, w_ada, b_ada).reshape(depth, ADA_ROWS, N_MOD, d)
    tables = _rope_tables(t)
    last_attn = max(i for i in range(depth) if i % N_MIXERS == 0)

    xl = x.reshape(b * t, d)
    xc = ctx.reshape(b * lc, d)
    for i in range(depth):
        ctx_update = i < last_attn
        is_attn = i % N_MIXERS == 0
        ctx_read = i <= last_attn and is_attn
        ml = [mods[i, :b, k].reshape(b, 1, d) for k in range(N_MOD)]
        mc = [mods[i, b:b + 1, k].reshape(1, 1, d) for k in range(N_MOD)]
        w_up = _col_tiles(ffn_w_up[i], TN)
        w_down = _col_tiles(ffn_w_down[i], TN)
        if is_attn:
            a = i // N_MIXERS
            lam_init = 0.8 - 0.6 * math.exp(-0.3 * i)
            w_qkv = _col_tiles(attn_w_qkv[a], TN)
            w_o = _col_tiles(attn_w_o[a], attn_w_o.shape[2])
            lams = [p[a].reshape(1, HEAD_DIM) for p in
                    (attn_lambda_q1, attn_lambda_k1, attn_lambda_q2, attn_lambda_k2)]
            gain = attn_subln_gain[a].reshape(1, V_HEAD_DIM)
            qt_l, k_l, vt_l = _qkv_call(xl, ml[0], ml[1], w_qkv, tables, n_seq=b, with_q=True, rope=True, **lat)
            srcs = [(k_l.reshape(b, t, QK_WIDTH), vt_l)]
            if ctx_read:
                outs_c = _qkv_call(xc, mc[0], mc[1], w_qkv, tables, n_seq=b, with_q=ctx_update, rope=False, **cx)
                k_c, vt_c = outs_c[-2].reshape(b, lc, QK_WIDTH), outs_c[-1]
                srcs.append((k_c, vt_c))
            o_l = _attn_call(qt_l, srcs, lams, gain, lam_init=lam_init)
            xl = _proj_res_call(o_l, w_o, xl, ml[2], **lat)
            if ctx_update:
                o_c = _attn_call(outs_c[0], [(k_c, vt_c)], lams, gain, lam_init=lam_init)
                xc = _proj_res_call(o_c, w_o, xc, mc[2], **cx)
        else:
            jx = i // N_MIXERS
            w_in = _col_tiles(sconv_w_in[jx], TN)
            w_out = _col_tiles(sconv_w_out[jx], sconv_w_out.shape[2])
            z_l = _sconv_in_call(xl, ml[0], ml[1], w_in, sconv_conv[jx], **lat_seq)
            xl = _proj_res_call(z_l, w_out, xl, ml[2], **lat)
            if ctx_update:
                z_c = _sconv_in_call(xc, mc[0], mc[1], w_in, sconv_conv[jx], **cx_seq)
                xc = _proj_res_call(z_c, w_out, xc, mc[2], **cx)
        u_l = _ffn_up_call(xl, ml[3], ml[4], w_up, ffn_conv[i], **lat_seq)
        xl = _proj_res_call(u_l, w_down, xl, ml[5], **lat)
        if ctx_update:
            u_c = _ffn_up_call(xc, mc[3], mc[4], w_up, ffn_conv[i], **cx_seq)
            xc = _proj_res_call(u_c, w_down, xc, mc[5], **cx)
    return _final_norm_call(xl, final_gain, tm=tm_l).reshape(b, t, d)
```

```python
import functools
import math

import jax
import jax.numpy as jnp
from jax import lax
from jax.experimental import pallas as pl
from jax.experimental.pallas import tpu as pltpu

F32 = jnp.float32
BF16 = jnp.bfloat16

GRID_W = 64
N_MIXERS = 2
DIFF_HEADS = 8
HEAD_DIM = 128
V_HEAD_DIM = 2 * HEAD_DIM
QK_WIDTH = DIFF_HEADS * 2 * HEAD_DIM
V_WIDTH = DIFF_HEADS * V_HEAD_DIM
ROPE_THETA = 10000.0
N_MOD = 6
EPS = 1e-6

LANES = 128
HALO = 16
V7X_VMEM_LIMIT = 56 * 1024 * 1024
ADA_ROWS = 8
TN = 512
SUB = 512
Q_SCALE = HEAD_DIM ** -0.5 * math.log2(math.e)


def _pick(n, candidates):
    for c in candidates:
        if n % c == 0:
            return c
    return n


def _params(*sem):
    return pltpu.CompilerParams(dimension_semantics=sem, vmem_limit_bytes=V7X_VMEM_LIMIT)


def _norm_mod(x, shift, scale):
    ms = jnp.mean(x * x, axis=-1, keepdims=True)
    return (x * lax.rsqrt(ms + EPS)) * (1.0 + scale) + shift


def _ada_kernel(c_ref, w_ref, b_ref, o_ref):
    c = c_ref[...]
    s = (c * jax.nn.sigmoid(c)).astype(BF16)
    o_ref[0] = jnp.dot(s, w_ref[0].astype(BF16), preferred_element_type=F32) + b_ref[0]


def _ada_call(cond, w_ada, b_ada):
    depth, d, n = w_ada.shape
    tn = _pick(n, (1024, 512, 256, 128))
    return pl.pallas_call(
        _ada_kernel,
        grid=(depth, n // tn),
        in_specs=[
            pl.BlockSpec((ADA_ROWS, d), lambda l, j: (0, 0)),
            pl.BlockSpec((1, d, tn), lambda l, j: (l, 0, j)),
            pl.BlockSpec((1, 1, tn), lambda l, j: (l, 0, j)),
        ],
        out_specs=pl.BlockSpec((1, ADA_ROWS, tn), lambda l, j: (l, 0, j)),
        out_shape=jax.ShapeDtypeStruct((depth, ADA_ROWS, n), F32),
        compiler_params=_params("arbitrary", "arbitrary"),
        name="ada",
    )(cond, w_ada, b_ada.reshape(depth, 1, n))


def _rope(y, cos, sin_lo, sin_hi):
    return y * cos + pltpu.roll(y, LANES - 32, 1) * sin_lo + pltpu.roll(y, 32, 1) * sin_hi


def _qkv_kernel(x_ref, sh_ref, sc_ref, cos_ref, slo_ref, shi_ref, *rest, tn, with_q, rope):
    if with_q:
        wq_ref, wk_ref, wv_ref, qt_ref, k_ref, vt_ref, h_scr = rest
    else:
        wk_ref, wv_ref, k_ref, vt_ref, h_scr = rest

    @pl.when(pl.program_id(1) == 0)
    def _():
        h_scr[...] = _norm_mod(x_ref[...], sh_ref[0], sc_ref[0]).astype(BF16)

    h = h_scr[...]

    def groups(y):
        for g in range(tn // LANES):
            yg = y[:, g * LANES:(g + 1) * LANES]
            if rope:
                yg = _rope(yg, cos_ref[...], slo_ref[...], shi_ref[...])
            yield g, yg

    if with_q:
        for g, yg in groups(jnp.dot(h, wq_ref[...], preferred_element_type=F32)):
            qt_ref[g * LANES:(g + 1) * LANES, :] = (yg * Q_SCALE).T.astype(BF16)
    for g, yg in groups(jnp.dot(h, wk_ref[...], preferred_element_type=F32)):
        k_ref[:, g * LANES:(g + 1) * LANES] = yg.astype(BF16)
    vt_ref[...] = jnp.dot(h, wv_ref[...], preferred_element_type=F32).T.astype(BF16)


def _qkv_call(x, shift, scale, w, tables, *, n_seq, tm, tiles_per_mod, with_q, rope):
    m, d = x.shape
    tn = TN
    assert QK_WIDTH == V_WIDTH
    seq = m // n_seq
    tps = seq // tm
    nt = QK_WIDTH // tn
    cos, slo, shi = tables
    pos_tiles = max(cos.shape[0] // tm, 1)
    tbl = pl.BlockSpec((tm, LANES), lambda i, j: (i % pos_tiles, 0))
    mod = pl.BlockSpec((1, 1, d), lambda i, j: (i // tiles_per_mod, 0, 0))
    wspec = lambda off: pl.BlockSpec((d, tn), lambda i, j: (0, j + off))
    w_specs = [wspec(nt), wspec(2 * nt)]
    out_specs = [
        pl.BlockSpec((tm, tn), lambda i, j: (i, j)),
        pl.BlockSpec((None, None, tn, tm), lambda i, j: (i // tps, i % tps, j, 0)),
    ]
    out_shape = [
        jax.ShapeDtypeStruct((m, QK_WIDTH), BF16),
        jax.ShapeDtypeStruct((n_seq, tps, V_WIDTH, tm), BF16),
    ]
    if with_q:
        w_specs.insert(0, wspec(0))
        out_specs.insert(0, pl.BlockSpec((None, tn, tm), lambda i, j: (i // tps, j, i % tps)))
        out_shape.insert(0, jax.ShapeDtypeStruct((n_seq, QK_WIDTH, seq), BF16))
    return pl.pallas_call(
        functools.partial(_qkv_kernel, tn=tn, with_q=with_q, rope=rope),
        grid=(m // tm, nt),
        in_specs=[pl.BlockSpec((tm, d), lambda i, j: (i, 0)), mod, mod, tbl, tbl, tbl] + w_specs,
        out_specs=out_specs,
        out_shape=out_shape,
        scratch_shapes=[pltpu.VMEM((tm, d), BF16)],
        compiler_params=_params("parallel", "arbitrary"),
        name="qkv",
    )(x, shift, scale, cos, slo, shi, *([w] * len(w_specs)))


def _attn_kernel(lq1_ref, lk1_ref, lq2_ref, lk2_ref, gain_ref, qt_ref, *rest, chunks, lam_init):
    n_src = len(chunks)
    srcs = [(rest[2 * s], rest[2 * s + 1]) for s in range(n_src)]
    o_ref, acc0, acc1, sa0, sa1, sb0, sb1 = rest[2 * n_src:]
    qt = qt_ref[...]
    q0t = qt[:HEAD_DIM, :]
    q1t = qt[HEAD_DIM:, :]
    tq = qt.shape[1]
    acc0[...] = jnp.zeros_like(acc0)
    acc1[...] = jnp.zeros_like(acc1)

    def scores(k):
        return (jnp.dot(k[:, :HEAD_DIM], q0t, preferred_element_type=F32),
                jnp.dot(k[:, HEAD_DIM:], q1t, preferred_element_type=F32))

    def update(s, mc, vt, acc, m, l):
        m_new = jnp.maximum(m, mc)
        alpha = jnp.exp2(m - m_new)
        p = jnp.exp2(s - m_new)
        l_new = alpha * l + jnp.sum(p, axis=0, keepdims=True)
        acc[...] = alpha * acc[...] + jnp.dot(vt, p.astype(BF16), preferred_element_type=F32)
        return m_new, l_new

    def colmax(s):
        return jnp.max(s, axis=0, keepdims=True)

    def step(k, vt, carry):
        m0, l0, m1, l1 = carry
        s0, s1 = scores(k)
        m0, l0 = update(s0, colmax(s0), vt, acc0, m0, l0)
        m1, l1 = update(s1, colmax(s1), vt, acc1, m1, l1)
        return m0, l0, m1, l1

    def stage(k, s0_scr, s1_scr):
        s0, s1 = scores(k)
        s0_scr[...] = s0
        s1_scr[...] = s1
        return colmax(s0), colmax(s1)

    def consume(s0_scr, s1_scr, mc, vt, carry):
        m0, l0, m1, l1 = carry
        m0, l0 = update(s0_scr[...], mc[0], vt, acc0, m0, l0)
        m1, l1 = update(s1_scr[...], mc[1], vt, acc1, m1, l1)
        return m0, l0, m1, l1

    neg = jnp.full((1, tq), -jnp.inf, F32)
    zero = jnp.zeros((1, tq), F32)
    carry = (neg, zero, neg, zero)
    for (k_ref, vt_ref), (n_chunks, tk) in zip(srcs, chunks):
        chunk = lambda c, k_ref=k_ref, tk=tk: k_ref[pl.ds(pl.multiple_of(c * tk, tk), tk), :]
        if n_chunks == 1:
            carry = step(k_ref[...], vt_ref[0], carry)
        elif n_chunks % 2 == 0 and sa0.shape[0] == tk:
            def body(i, state, chunk=chunk, vt_ref=vt_ref):
                carry, mca = state[:4], state[4:]
                c = 2 * i
                mcb = stage(chunk(c + 1), sb0, sb1)
                carry = consume(sa0, sa1, mca, vt_ref[c], carry)
                mca = stage(chunk(c + 2), sa0, sa1)
                carry = consume(sb0, sb1, mcb, vt_ref[c + 1], carry)
                return carry + mca
            mca = stage(chunk(0), sa0, sa1)
            state = lax.fori_loop(0, n_chunks // 2 - 1, body, carry + mca)
            carry, mca = state[:4], state[4:]
            mcb = stage(chunk(n_chunks - 1), sb0, sb1)
            carry = consume(sa0, sa1, mca, vt_ref[n_chunks - 2], carry)
            carry = consume(sb0, sb1, mcb, vt_ref[n_chunks - 1], carry)
        else:
            carry = lax.fori_loop(
                0, n_chunks, lambda c, carry, chunk=chunk, vt_ref=vt_ref: step(chunk(c), vt_ref[c], carry), carry)
    _, l0, _, l1 = carry

    lam = (jnp.exp(jnp.sum(lq1_ref[...] * lk1_ref[...], axis=-1, keepdims=True))
           - jnp.exp(jnp.sum(lq2_ref[...] * lk2_ref[...], axis=-1, keepdims=True)) + lam_init)
    ot = acc0[...] / l0 - lam * (acc1[...] / l1)
    ms = jnp.mean(ot * ot, axis=0, keepdims=True)
    o = (ot * lax.rsqrt(ms + EPS)).T * gain_ref[...] * (1.0 - lam_init)
    o_ref[...] = o.astype(o_ref.dtype)


def _attn_call(qt, srcs, lams, gain, *, lam_init):
    n_seq, _, n_q = qt.shape
    tq = _pick(n_q, (512, 256, 128))
    nq = n_q // tq
    hw = V_HEAD_DIM
    lam_spec = pl.BlockSpec((1, HEAD_DIM), lambda b, h, i: (0, 0))
    in_specs = [
        lam_spec, lam_spec, lam_spec, lam_spec,
        pl.BlockSpec((1, hw), lambda b, h, i: (0, 0)),
        pl.BlockSpec((None, hw, tq), lambda b, h, i: (b, h, i)),
    ]
    args, chunks = [], []
    for k, vt in srcs:
        n_chunks, tk = vt.shape[1], vt.shape[3]
        chunks.append((n_chunks, tk))
        in_specs.append(pl.BlockSpec((None, k.shape[1], hw), lambda b, h, i: (b, 0, h)))
        in_specs.append(pl.BlockSpec((None, n_chunks, hw, tk), lambda b, h, i: (b, 0, h, 0)))
        args += [k, vt]
    stage_tk = max(chunks)[1] if max(chunks)[0] > 1 else 8
    s_scr = pltpu.VMEM((stage_tk, tq), F32)
    return pl.pallas_call(
        functools.partial(_attn_kernel, chunks=tuple(chunks), lam_init=lam_init),
        grid=(n_seq, DIFF_HEADS, nq),
        in_specs=in_specs,
        out_specs=pl.BlockSpec((tq, hw), lambda b, h, i: (b * nq + i, h)),
        out_shape=jax.ShapeDtypeStruct((n_seq * n_q, V_WIDTH), BF16),
        scratch_shapes=[pltpu.VMEM((hw, tq), F32), pltpu.VMEM((hw, tq), F32), s_scr, s_scr, s_scr, s_scr],
        compiler_params=_params("parallel", "parallel", "arbitrary"),
        name="diff_attn",
    )(*lams, gain, qt, *args)


def _fill_h(h_scr, x_ref, xp_ref, xn_ref, sh, sc, first, last, tm):
    hp = _norm_mod(xp_ref[...], sh, sc)
    hn = _norm_mod(xn_ref[...], sh, sc)
    h_scr[0:HALO, :] = jnp.where(first, 0.0, hp).astype(BF16)
    h_scr[HALO:HALO + tm, :] = _norm_mod(x_ref[...], sh, sc).astype(BF16)
    h_scr[HALO + tm:, :] = jnp.where(last, 0.0, hn).astype(BF16)


def _conv3(y_scr, w_ref, tm):
    return (y_scr[pl.ds(HALO - 1, tm), :] * w_ref[0:1, :]
            + y_scr[pl.ds(HALO, tm), :] * w_ref[1:2, :]
            + y_scr[pl.ds(HALO + 1, tm), :] * w_ref[2:3, :])


def _halo_specs(tm, d, m):
    r = tm // HALO
    last_blk = m // HALO - 1
    return [
        pl.BlockSpec((tm, d), lambda i, j: (i, 0)),
        pl.BlockSpec((HALO, d), lambda i, j: (jnp.maximum(i * r - 1, 0), 0)),
        pl.BlockSpec((HALO, d), lambda i, j: (jnp.minimum((i + 1) * r, last_blk), 0)),
    ]


def _conv_prologue(x_ref, xp_ref, xn_ref, sh_ref, sc_ref, h_scr, *, tm, tiles_per_seq):
    @pl.when(pl.program_id(1) == 0)
    def _():
        t = pl.program_id(0) % tiles_per_seq
        _fill_h(h_scr, x_ref, xp_ref, xn_ref, sh_ref[0], sc_ref[0], t == 0, t == tiles_per_seq - 1, tm)


def _sconv_in_kernel(x_ref, xp_ref, xn_ref, sh_ref, sc_ref, wb_ref, wc_ref, wx_ref, cw_ref, o_ref,
                     h_scr, *p_scrs, tm, tiles_per_seq):
    _conv_prologue(x_ref, xp_ref, xn_ref, sh_ref, sc_ref, h_scr, tm=tm, tiles_per_seq=tiles_per_seq)
    h = h_scr[...]
    for c, p_scr in enumerate(p_scrs):
        cols = slice(c * SUB, (c + 1) * SUB)
        p_scr[...] = (jnp.dot(h, wc_ref[:, cols], preferred_element_type=F32)
                      * jnp.dot(h, wx_ref[:, cols], preferred_element_type=F32))
        gb = jnp.dot(h_scr[HALO:HALO + tm, :], wb_ref[:, cols], preferred_element_type=F32)
        o_ref[:, cols] = (gb * _conv3(p_scr, cw_ref.at[:, cols], tm)).astype(o_ref.dtype)


def _sconv_in_call(x, shift, scale, w, conv_w, *, tm, tiles_per_mod, tiles_per_seq):
    m, d = x.shape
    tn = TN
    nt = w.shape[1] // 3 // tn
    mod = pl.BlockSpec((1, 1, d), lambda i, j: (i // tiles_per_mod, 0, 0))
    wspec = lambda off: pl.BlockSpec((d, tn), lambda i, j: (0, j + off))
    ext = tm + 2 * HALO
    return pl.pallas_call(
        functools.partial(_sconv_in_kernel, tm=tm, tiles_per_seq=tiles_per_seq),
        grid=(m // tm, nt),
        in_specs=_halo_specs(tm, d, m) + [
            mod, mod, wspec(0), wspec(nt), wspec(2 * nt),
            pl.BlockSpec((3, tn), lambda i, j: (0, j)),
        ],
        out_specs=pl.BlockSpec((tm, tn), lambda i, j: (i, j)),
        out_shape=jax.ShapeDtypeStruct((m, nt * tn), BF16),
        scratch_shapes=[pltpu.VMEM((ext, d), BF16)] + [pltpu.VMEM((ext, SUB), F32)] * (tn // SUB),
        compiler_params=_params("parallel", "arbitrary"),
        name="sconv_in",
    )(x, x, x, shift, scale, w, w, w, conv_w)


def _ffn_up_kernel(x_ref, xp_ref, xn_ref, sh_ref, sc_ref, wg_ref, wv_ref, cg_ref, cv_ref, o_ref,
                   h_scr, *y_scrs, tm, tiles_per_seq):
    _conv_prologue(x_ref, xp_ref, xn_ref, sh_ref, sc_ref, h_scr, tm=tm, tiles_per_seq=tiles_per_seq)
    h = h_scr[...]
    for c in range(len(y_scrs) // 2):
        cols = slice(c * SUB, (c + 1) * SUB)
        yg_scr, yv_scr = y_scrs[2 * c], y_scrs[2 * c + 1]
        yg_scr[...] = jnp.dot(h, wg_ref[:, cols], preferred_element_type=F32)
        yv_scr[...] = jnp.dot(h, wv_ref[:, cols], preferred_element_type=F32)
        ug = _conv3(yg_scr, cg_ref.at[:, cols], tm)
        uv = _conv3(yv_scr, cv_ref.at[:, cols], tm)
        o_ref[:, cols] = (ug * jax.nn.sigmoid(ug) * uv).astype(o_ref.dtype)


def _ffn_up_call(x, shift, scale, w, conv_w, *, tm, tiles_per_mod, tiles_per_seq):
    m, d = x.shape
    tn = TN
    nt = w.shape[1] // 2 // tn
    mod = pl.BlockSpec((1, 1, d), lambda i, j: (i // tiles_per_mod, 0, 0))
    wspec = lambda off: pl.BlockSpec((d, tn), lambda i, j: (0, j + off))
    cspec = lambda off: pl.BlockSpec((3, tn), lambda i, j: (0, j + off))
    ext = tm + 2 * HALO
    return pl.pallas_call(
        functools.partial(_ffn_up_kernel, tm=tm, tiles_per_seq=tiles_per_seq),
        grid=(m // tm, nt),
        in_specs=_halo_specs(tm, d, m) + [mod, mod, wspec(0), wspec(nt), cspec(0), cspec(nt)],
        out_specs=pl.BlockSpec((tm, tn), lambda i, j: (i, j)),
        out_shape=jax.ShapeDtypeStruct((m, nt * tn), BF16),
        scratch_shapes=[pltpu.VMEM((ext, d), BF16)] + [pltpu.VMEM((ext, SUB), F32)] * (2 * tn // SUB),
        compiler_params=_params("parallel", "arbitrary"),
        name="ffn_up",
    )(x, x, x, shift, scale, w, w, conv_w, conv_w)


def _proj_res_kernel(a_ref, w_ref, res_ref, g_ref, o_ref):
    y = jnp.dot(a_ref[...], w_ref[...], preferred_element_type=F32)
    o_ref[...] = res_ref[...] + g_ref[0] * y


def _proj_res_call(a, w, res, gate, *, tm, tiles_per_mod):
    m, k = a.shape
    n = w.shape[1]
    tn = n if k * n * 2 <= 8 * 1024 * 1024 else TN
    return pl.pallas_call(
        _proj_res_kernel,
        grid=(m // tm, n // tn),
        in_specs=[
            pl.BlockSpec((tm, k), lambda i, j: (i, 0)),
            pl.BlockSpec((k, tn), lambda i, j: (0, j)),
            pl.BlockSpec((tm, tn), lambda i, j: (i, j)),
            pl.BlockSpec((1, 1, tn), lambda i, j: (i // tiles_per_mod, 0, j)),
        ],
        out_specs=pl.BlockSpec((tm, tn), lambda i, j: (i, j)),
        out_shape=jax.ShapeDtypeStruct((m, n), F32),
        compiler_params=_params("parallel", "arbitrary"),
        name="proj_res",
    )(a, w, res, gate)


def _final_norm_kernel(x_ref, g_ref, o_ref):
    x = x_ref[...]
    ms = jnp.mean(x * x, axis=-1, keepdims=True)
    o_ref[...] = (x * lax.rsqrt(ms + EPS)) * g_ref[...]


def _final_norm_call(x, gain, *, tm):
    m, d = x.shape
    return pl.pallas_call(
        _final_norm_kernel,
        grid=(m // tm,),
        in_specs=[pl.BlockSpec((tm, d), lambda i: (i, 0)), pl.BlockSpec((1, d), lambda i: (0, 0))],
        out_specs=pl.BlockSpec((tm, d), lambda i: (i, 0)),
        out_shape=jax.ShapeDtypeStruct((m, d), F32),
        compiler_params=_params("parallel"),
        name="final_norm",
    )(x, gain.reshape(1, d))


def _rope_tables(n_tokens):
    rows = n_tokens // GRID_W
    row = jnp.repeat(jnp.arange(rows, dtype=F32), GRID_W)
    col = jnp.tile(jnp.arange(GRID_W, dtype=F32), rows)
    half = HEAD_DIM // 2
    inv_freq = 1.0 / (ROPE_THETA ** (jnp.arange(0, half, 2, dtype=F32) / half))
    ang_r = row[:, None] * inv_freq
    ang_c = col[:, None] * inv_freq
    zeros = jnp.zeros_like(ang_r)
    cos = jnp.concatenate([jnp.cos(ang_r)] * 2 + [jnp.cos(ang_c)] * 2, axis=-1)
    sin_lo = jnp.concatenate([-jnp.sin(ang_r), zeros, -jnp.sin(ang_c), zeros], axis=-1)
    sin_hi = jnp.concatenate([zeros, jnp.sin(ang_r), zeros, jnp.sin(ang_c)], axis=-1)
    return cos, sin_lo, sin_hi


def kernel(x, c, ctx, c_ctx, w_ada, b_ada, attn_w_qkv, attn_w_o, attn_lambda_q1, attn_lambda_k1,
           attn_lambda_q2, attn_lambda_k2, attn_subln_gain, sconv_w_in, sconv_conv, sconv_w_out,
           ffn_w_up, ffn_conv, ffn_w_down, final_gain):
    b, t, d = x.shape
    lc = ctx.shape[1]
    depth = w_ada.shape[0]
    assert b + 1 <= ADA_ROWS and t % GRID_W == 0
    tm_l = _pick(t, (1024, 512, 256, 128))
    tm_c = _pick(lc, (256, 128))
    lat = dict(tm=tm_l, tiles_per_mod=t // tm_l)
    cx = dict(tm=tm_c, tiles_per_mod=(b * lc) // tm_c)
    lat_seq = dict(tiles_per_seq=t // tm_l, **lat)
    cx_seq = dict(tiles_per_seq=lc // tm_c, **cx)

    cond = jnp.zeros((ADA_ROWS, d), F32).at[:b].set(c).at[b].set(c_ctx)
    mods = _ada_call(cond, w_ada, b_ada).reshape(depth, ADA_ROWS, N_MOD, d)
    tables = _rope_tables(t)
    last_attn = max(i for i in range(depth) if i % N_MIXERS == 0)

    xl = x.reshape(b * t, d)
    xc = ctx.reshape(b * lc, d)
    for i in range(depth):
        ctx_update = i < last_attn
        is_attn = i % N_MIXERS == 0
        ctx_read = i <= last_attn and is_attn
        ml = [mods[i, :b, k].reshape(b, 1, d) for k in range(N_MOD)]
        mc = [mods[i, b:b + 1, k].reshape(1, 1, d) for k in range(N_MOD)]
        w_up = ffn_w_up[i].astype(BF16)
        w_down = ffn_w_down[i].astype(BF16)
        if is_attn:
            a = i // N_MIXERS
            lam_init = 0.8 - 0.6 * math.exp(-0.3 * i)
            w_qkv = attn_w_qkv[a].astype(BF16)
            w_o = attn_w_o[a].astype(BF16)
            lams = [p[a].reshape(1, HEAD_DIM) for p in
                    (attn_lambda_q1, attn_lambda_k1, attn_lambda_q2, attn_lambda_k2)]
            gain = attn_subln_gain[a].reshape(1, V_HEAD_DIM)
            qt_l, k_l, vt_l = _qkv_call(xl, ml[0], ml[1], w_qkv, tables, n_seq=b, with_q=True, rope=True, **lat)
            srcs = [(k_l.reshape(b, t, QK_WIDTH), vt_l)]
            if ctx_read:
                outs_c = _qkv_call(xc, mc[0], mc[1], w_qkv, tables, n_seq=b, with_q=ctx_update, rope=False, **cx)
                k_c, vt_c = outs_c[-2].reshape(b, lc, QK_WIDTH), outs_c[-1]
                srcs.append((k_c, vt_c))
            o_l = _attn_call(qt_l, srcs, lams, gain, lam_init=lam_init)
            xl = _proj_res_call(o_l, w_o, xl, ml[2], **lat)
            if ctx_update:
                o_c = _attn_call(outs_c[0], [(k_c, vt_c)], lams, gain, lam_init=lam_init)
                xc = _proj_res_call(o_c, w_o, xc, mc[2], **cx)
        else:
            jx = i // N_MIXERS
            w_in = sconv_w_in[jx].astype(BF16)
            w_out = sconv_w_out[jx].astype(BF16)
            z_l = _sconv_in_call(xl, ml[0], ml[1], w_in, sconv_conv[jx], **lat_seq)
            xl = _proj_res_call(z_l, w_out, xl, ml[2], **lat)
            if ctx_update:
                z_c = _sconv_in_call(xc, mc[0], mc[1], w_in, sconv_conv[jx], **cx_seq)
                xc = _proj_res_call(z_c, w_out, xc, mc[2], **cx)
        u_l = _ffn_up_call(xl, ml[3], ml[4], w_up, ffn_conv[i], **lat_seq)
        xl = _proj_res_call(u_l, w_down, xl, ml[5], **lat)
        if ctx_update:
            u_c = _ffn_up_call(xc, mc[3], mc[4], w_up, ffn_conv[i], **cx_seq)
            xc = _proj_res_call(u_c, w_down, xc, mc[5], **cx)
    return _final_norm_call(xl, final_gain, tm=tm_l).reshape(b, t, d)
```

```python
import functools
import math

import jax
import jax.numpy as jnp
from jax import lax
from jax.experimental import pallas as pl
from jax.experimental.pallas import tpu as pltpu

F32 = jnp.float32
BF16 = jnp.bfloat16

GRID_W = 64
N_MIXERS = 2
DIFF_HEADS = 8
HEAD_DIM = 128
V_HEAD_DIM = 2 * HEAD_DIM
QK_WIDTH = DIFF_HEADS * 2 * HEAD_DIM
V_WIDTH = DIFF_HEADS * V_HEAD_DIM
ROPE_THETA = 10000.0
N_MOD = 6
EPS = 1e-6

LANES = 128
SUBLANES = 8
HALO = 16
V7X_VMEM_LIMIT = 56 * 1024 * 1024
ADA_ROWS = 8
TN = 512
Q_SCALE = HEAD_DIM ** -0.5 * math.log2(math.e)


def _pick(n, candidates):
    for c in candidates:
        if n % c == 0:
            return c
    return n


def _params(*sem):
    return pltpu.CompilerParams(dimension_semantics=sem, vmem_limit_bytes=V7X_VMEM_LIMIT)


def _norm_mod(x, shift, scale):
    ms = jnp.mean(x * x, axis=-1, keepdims=True)
    return (x * lax.rsqrt(ms + EPS)) * (1.0 + scale) + shift


def _ada_kernel(c_ref, w_ref, b_ref, o_ref):
    c = c_ref[...]
    s = (c * jax.nn.sigmoid(c)).astype(BF16)
    o_ref[0] = jnp.dot(s, w_ref[0].astype(BF16), preferred_element_type=F32) + b_ref[0]


def _ada_call(cond, w_ada, b_ada):
    depth, d, n = w_ada.shape
    tn = _pick(n, (1024, 512, 256, 128))
    return pl.pallas_call(
        _ada_kernel,
        grid=(depth, n // tn),
        in_specs=[
            pl.BlockSpec((ADA_ROWS, d), lambda l, j: (0, 0)),
            pl.BlockSpec((1, d, tn), lambda l, j: (l, 0, j)),
            pl.BlockSpec((1, 1, tn), lambda l, j: (l, 0, j)),
        ],
        out_specs=pl.BlockSpec((1, ADA_ROWS, tn), lambda l, j: (l, 0, j)),
        out_shape=jax.ShapeDtypeStruct((depth, ADA_ROWS, n), F32),
        compiler_params=_params("arbitrary", "arbitrary"),
        name="ada",
    )(cond, w_ada, b_ada.reshape(depth, 1, n))


def _rope(y, cos, sin_lo, sin_hi):
    return y * cos + pltpu.roll(y, LANES - 32, 1) * sin_lo + pltpu.roll(y, 32, 1) * sin_hi


def _qkv_kernel(x_ref, sh_ref, sc_ref, cos_ref, slo_ref, shi_ref, *rest, tn, with_q, rope):
    if with_q:
        wq_ref, wk_ref, wv_ref, qt_ref, k_ref, vt_ref, h_scr = rest
    else:
        wk_ref, wv_ref, k_ref, vt_ref, h_scr = rest

    @pl.when(pl.program_id(1) == 0)
    def _():
        h_scr[...] = _norm_mod(x_ref[...], sh_ref[0], sc_ref[0]).astype(BF16)

    h = h_scr[...]

    def groups(y):
        for g in range(tn // LANES):
            yg = y[:, g * LANES:(g + 1) * LANES]
            if rope:
                yg = _rope(yg, cos_ref[...], slo_ref[...], shi_ref[...])
            yield g, yg

    if with_q:
        for g, yg in groups(jnp.dot(h, wq_ref[...], preferred_element_type=F32)):
            qt_ref[g * LANES:(g + 1) * LANES, :] = (yg * Q_SCALE).T.astype(BF16)
    for g, yg in groups(jnp.dot(h, wk_ref[...], preferred_element_type=F32)):
        k_ref[:, g * LANES:(g + 1) * LANES] = yg.astype(BF16)
    vt_ref[...] = jnp.dot(h, wv_ref[...], preferred_element_type=F32).T.astype(BF16)


def _qkv_call(x, shift, scale, w_layer, tables, *, n_seq, tm, tiles_per_mod, with_q, rope):
    m, d = x.shape
    tn = TN
    assert QK_WIDTH == V_WIDTH
    seq = m // n_seq
    tps = seq // tm
    nt = QK_WIDTH // tn
    cos, slo, shi = tables
    pos_tiles = max(cos.shape[0] // tm, 1)
    tbl = pl.BlockSpec((tm, LANES), lambda i, j: (i % pos_tiles, 0))
    mod = pl.BlockSpec((1, 1, d), lambda i, j: (i // tiles_per_mod, 0, 0))
    wspec = lambda off: _wspec(w_layer, d, tn, off)
    w_specs = [wspec(nt), wspec(2 * nt)]
    out_specs = [
        pl.BlockSpec((tm, tn), lambda i, j: (i, j)),
        pl.BlockSpec((None, None, tn, tm), lambda i, j: (i // tps, i % tps, j, 0)),
    ]
    out_shape = [
        jax.ShapeDtypeStruct((m, QK_WIDTH), BF16),
        jax.ShapeDtypeStruct((n_seq, tps, V_WIDTH, tm), BF16),
    ]
    if with_q:
        w_specs.insert(0, wspec(0))
        out_specs.insert(0, pl.BlockSpec((None, tn, tm), lambda i, j: (i // tps, j, i % tps)))
        out_shape.insert(0, jax.ShapeDtypeStruct((n_seq, QK_WIDTH, seq), BF16))
    return pl.pallas_call(
        functools.partial(_qkv_kernel, tn=tn, with_q=with_q, rope=rope),
        grid=(m // tm, nt),
        in_specs=[pl.BlockSpec((tm, d), lambda i, j: (i, 0)), mod, mod, tbl, tbl, tbl] + w_specs,
        out_specs=out_specs,
        out_shape=out_shape,
        scratch_shapes=[pltpu.VMEM((tm, d), BF16)],
        compiler_params=_params("parallel", "arbitrary"),
        name="qkv",
    )(x, shift, scale, cos, slo, shi, *([w_layer[0]] * len(w_specs)))


def _attn_kernel(lq1_ref, lk1_ref, lq2_ref, lk2_ref, gain_ref, qt_ref, *rest, chunks, lam_init):
    n_src = len(chunks)
    srcs = [(rest[2 * s], rest[2 * s + 1]) for s in range(n_src)]
    o_ref, acc0, acc1, sa0, sa1, sb0, sb1 = rest[2 * n_src:]
    qt = qt_ref[...]
    q0t = qt[:HEAD_DIM, :]
    q1t = qt[HEAD_DIM:, :]
    tq = qt.shape[1]
    acc0[...] = jnp.zeros_like(acc0)
    acc1[...] = jnp.zeros_like(acc1)

    def scores(k):
        return (jnp.dot(k[:, :HEAD_DIM], q0t, preferred_element_type=F32),
                jnp.dot(k[:, HEAD_DIM:], q1t, preferred_element_type=F32))

    def update(s, mc, vt, acc, m, l):
        m_new = jnp.maximum(m, mc)
        alpha = jnp.exp2(m - m_new)
        p = jnp.exp2(s - m_new)
        l_new = alpha * l + jnp.sum(p, axis=0, keepdims=True)
        acc[...] = alpha * acc[...] + jnp.dot(vt, p.astype(BF16), preferred_element_type=F32)
        return m_new, l_new

    def colmax(s):
        return jnp.max(s, axis=0, keepdims=True)

    def step(k, vt, carry):
        m0, l0, m1, l1 = carry
        s0, s1 = scores(k)
        m0, l0 = update(s0, colmax(s0), vt, acc0, m0, l0)
        m1, l1 = update(s1, colmax(s1), vt, acc1, m1, l1)
        return m0, l0, m1, l1

    def stage(k, s0_scr, s1_scr):
        s0, s1 = scores(k)
        s0_scr[...] = s0
        s1_scr[...] = s1
        return colmax(s0), colmax(s1)

    def consume(s0_scr, s1_scr, mc, vt, carry):
        m0, l0, m1, l1 = carry
        m0, l0 = update(s0_scr[...], mc[0], vt, acc0, m0, l0)
        m1, l1 = update(s1_scr[...], mc[1], vt, acc1, m1, l1)
        return m0, l0, m1, l1

    neg = jnp.full((1, tq), -jnp.inf, F32)
    zero = jnp.zeros((1, tq), F32)
    carry = (neg, zero, neg, zero)
    for (k_ref, vt_ref), (n_chunks, tk) in zip(srcs, chunks):
        chunk = lambda c, k_ref=k_ref, tk=tk: k_ref[pl.ds(pl.multiple_of(c * tk, tk), tk), :]
        if n_chunks == 1:
            carry = step(k_ref[...], vt_ref[0], carry)
        elif n_chunks % 2 == 0 and sa0.shape[0] == tk:
            def body(i, state, chunk=chunk, vt_ref=vt_ref):
                carry, mca = state[:4], state[4:]
                c = 2 * i
                mcb = stage(chunk(c + 1), sb0, sb1)
                carry = consume(sa0, sa1, mca, vt_ref[c], carry)
                mca = stage(chunk(c + 2), sa0, sa1)
                carry = consume(sb0, sb1, mcb, vt_ref[c + 1], carry)
                return carry + mca
            mca = stage(chunk(0), sa0, sa1)
            state = lax.fori_loop(0, n_chunks // 2 - 1, body, carry + mca)
            carry, mca = state[:4], state[4:]
            mcb = stage(chunk(n_chunks - 1), sb0, sb1)
            carry = consume(sa0, sa1, mca, vt_ref[n_chunks - 2], carry)
            carry = consume(sb0, sb1, mcb, vt_ref[n_chunks - 1], carry)
        else:
            carry = lax.fori_loop(
                0, n_chunks, lambda c, carry, chunk=chunk, vt_ref=vt_ref: step(chunk(c), vt_ref[c], carry), carry)
    _, l0, _, l1 = carry

    lam = (jnp.exp(jnp.sum(lq1_ref[...] * lk1_ref[...], axis=-1, keepdims=True))
           - jnp.exp(jnp.sum(lq2_ref[...] * lk2_ref[...], axis=-1, keepdims=True)) + lam_init)
    ot = acc0[...] / l0 - lam * (acc1[...] / l1)
    ms = jnp.mean(ot * ot, axis=0, keepdims=True)
    o = (ot * lax.rsqrt(ms + EPS)).T * gain_ref[...] * (1.0 - lam_init)
    o_ref[...] = o.astype(o_ref.dtype)


def _attn_call(qt, srcs, lams, gain, *, lam_init):
    n_seq, _, n_q = qt.shape
    tq = _pick(n_q, (512, 256, 128))
    nq = n_q // tq
    hw = V_HEAD_DIM
    lam_spec = pl.BlockSpec((1, HEAD_DIM), lambda b, h, i: (0, 0))
    in_specs = [
        lam_spec, lam_spec, lam_spec, lam_spec,
        pl.BlockSpec((1, hw), lambda b, h, i: (0, 0)),
        pl.BlockSpec((None, hw, tq), lambda b, h, i: (b, h, i)),
    ]
    args, chunks = [], []
    for k, vt in srcs:
        n_chunks, tk = vt.shape[1], vt.shape[3]
        chunks.append((n_chunks, tk))
        in_specs.append(pl.BlockSpec((None, k.shape[1], hw), lambda b, h, i: (b, 0, h)))
        in_specs.append(pl.BlockSpec((None, n_chunks, hw, tk), lambda b, h, i: (b, 0, h, 0)))
        args += [k, vt]
    stage_tk = max(chunks)[1] if max(chunks)[0] > 1 else 8
    s_scr = pltpu.VMEM((stage_tk, tq), F32)
    return pl.pallas_call(
        functools.partial(_attn_kernel, chunks=tuple(chunks), lam_init=lam_init),
        grid=(n_seq, DIFF_HEADS, nq),
        in_specs=in_specs,
        out_specs=pl.BlockSpec((tq, hw), lambda b, h, i: (b * nq + i, h)),
        out_shape=jax.ShapeDtypeStruct((n_seq * n_q, V_WIDTH), BF16),
        scratch_shapes=[pltpu.VMEM((hw, tq), F32), pltpu.VMEM((hw, tq), F32), s_scr, s_scr, s_scr, s_scr],
        compiler_params=_params("parallel", "parallel", "arbitrary"),
        name="diff_attn",
    )(*lams, gain, qt, *args)


def _to_tiles(x, tm):
    m, d = x.shape
    return x.reshape(m // tm, SUBLANES, tm // SUBLANES, d).swapaxes(1, 2).reshape(m, d)


def _from_tiles(x, tm):
    m, d = x.shape
    return x.reshape(m // tm, tm // SUBLANES, SUBLANES, d).swapaxes(1, 2).reshape(m, d)


def _fill_h(h_scr, x_ref, xp_ref, xn_ref, sh, sc, first, last, tm):
    h = _norm_mod(x_ref[...], sh, sc)
    prev_tok = jnp.where(first, 0.0, _norm_mod(xp_ref[...], sh, sc)[HALO - 1:HALO, :])
    next_tok = jnp.where(last, 0.0, _norm_mod(xn_ref[...], sh, sc)[0:1, :])
    row = lax.broadcasted_iota(jnp.int32, (SUBLANES, h.shape[1]), 0)
    up = jnp.where(row == 0, prev_tok, pltpu.roll(h[tm - SUBLANES:, :], 1, 0))
    down = jnp.where(row == SUBLANES - 1, next_tok, pltpu.roll(h[:SUBLANES, :], SUBLANES - 1, 0))
    zeros = jnp.zeros_like(up)
    h_scr[0:HALO, :] = jnp.concatenate([zeros, up], axis=0).astype(BF16)
    h_scr[HALO:HALO + tm, :] = h.astype(BF16)
    h_scr[HALO + tm:, :] = jnp.concatenate([down, zeros], axis=0).astype(BF16)


def _conv3(y_scr, w_ref, tm):
    return (y_scr[pl.ds(HALO - SUBLANES, tm), :] * w_ref[0:1, :]
            + y_scr[pl.ds(HALO, tm), :] * w_ref[1:2, :]
            + y_scr[pl.ds(HALO + SUBLANES, tm), :] * w_ref[2:3, :])


def _halo_specs(tm, d, m):
    r = tm // HALO
    last_blk = m // HALO - 1
    return [
        pl.BlockSpec((tm, d), lambda i, j: (i, 0)),
        pl.BlockSpec((HALO, d), lambda i, j: (jnp.maximum(i * r - 1, 0), 0)),
        pl.BlockSpec((HALO, d), lambda i, j: (jnp.minimum((i + 1) * r, last_blk), 0)),
    ]


def _conv_prologue(x_ref, xp_ref, xn_ref, sh_ref, sc_ref, h_scr, *, tm, tiles_per_seq):
    @pl.when(pl.program_id(1) == 0)
    def _():
        t = pl.program_id(0) % tiles_per_seq
        _fill_h(h_scr, x_ref, xp_ref, xn_ref, sh_ref[0], sc_ref[0], t == 0, t == tiles_per_seq - 1, tm)


def _wspec(w_layer, d, tn, off):
    layer = w_layer[1]
    return pl.BlockSpec((None, d, tn), lambda i, j: (layer, 0, j + off))


def _sconv_in_kernel(x_ref, xp_ref, xn_ref, sh_ref, sc_ref, wb_ref, wc_ref, wx_ref, cw_ref, o_ref,
                     h_scr, p_scr, *, tm, tiles_per_seq):
    _conv_prologue(x_ref, xp_ref, xn_ref, sh_ref, sc_ref, h_scr, tm=tm, tiles_per_seq=tiles_per_seq)
    h = h_scr[...]
    p_scr[...] = (jnp.dot(h, wc_ref[...], preferred_element_type=F32)
                  * jnp.dot(h, wx_ref[...], preferred_element_type=F32))
    gb = jnp.dot(h_scr[HALO:HALO + tm, :], wb_ref[...], preferred_element_type=F32)
    o_ref[...] = (gb * _conv3(p_scr, cw_ref, tm)).astype(o_ref.dtype)


def _sconv_in_call(x, shift, scale, w_layer, conv_w, *, tm, tiles_per_mod, tiles_per_seq):
    m, d = x.shape
    w = w_layer[0]
    tn = TN
    nt = w.shape[2] // 3 // tn
    mod = pl.BlockSpec((1, 1, d), lambda i, j: (i // tiles_per_mod, 0, 0))
    ext = tm + 2 * HALO
    return pl.pallas_call(
        functools.partial(_sconv_in_kernel, tm=tm, tiles_per_seq=tiles_per_seq),
        grid=(m // tm, nt),
        in_specs=_halo_specs(tm, d, m) + [
            mod, mod, _wspec(w_layer, d, tn, 0), _wspec(w_layer, d, tn, nt), _wspec(w_layer, d, tn, 2 * nt),
            pl.BlockSpec((3, tn), lambda i, j: (0, j)),
        ],
        out_specs=pl.BlockSpec((tm, tn), lambda i, j: (i, j)),
        out_shape=jax.ShapeDtypeStruct((m, nt * tn), BF16),
        scratch_shapes=[pltpu.VMEM((ext, d), BF16), pltpu.VMEM((ext, tn), F32)],
        compiler_params=_params("parallel", "arbitrary"),
        name="sconv_in",
    )(x, x, x, shift, scale, w, w, w, conv_w)


def _ffn_up_kernel(x_ref, xp_ref, xn_ref, sh_ref, sc_ref, wg_ref, wv_ref, cg_ref, cv_ref, o_ref,
                   h_scr, yg_scr, yv_scr, *, tm, tiles_per_seq):
    _conv_prologue(x_ref, xp_ref, xn_ref, sh_ref, sc_ref, h_scr, tm=tm, tiles_per_seq=tiles_per_seq)
    h = h_scr[...]
    yg_scr[...] = jnp.dot(h, wg_ref[...], preferred_element_type=F32)
    yv_scr[...] = jnp.dot(h, wv_ref[...], preferred_element_type=F32)
    ug = _conv3(yg_scr, cg_ref, tm)
    uv = _conv3(yv_scr, cv_ref, tm)
    o_ref[...] = (ug * jax.nn.sigmoid(ug) * uv).astype(o_ref.dtype)


def _ffn_up_call(x, shift, scale, w_layer, conv_w, *, tm, tiles_per_mod, tiles_per_seq):
    m, d = x.shape
    w = w_layer[0]
    tn = TN
    nt = w.shape[2] // 2 // tn
    mod = pl.BlockSpec((1, 1, d), lambda i, j: (i // tiles_per_mod, 0, 0))
    cspec = lambda off: pl.BlockSpec((3, tn), lambda i, j: (0, j + off))
    ext = tm + 2 * HALO
    return pl.pallas_call(
        functools.partial(_ffn_up_kernel, tm=tm, tiles_per_seq=tiles_per_seq),
        grid=(m // tm, nt),
        in_specs=_halo_specs(tm, d, m) + [
            mod, mod, _wspec(w_layer, d, tn, 0), _wspec(w_layer, d, tn, nt), cspec(0), cspec(nt)],
        out_specs=pl.BlockSpec((tm, tn), lambda i, j: (i, j)),
        out_shape=jax.ShapeDtypeStruct((m, nt * tn), BF16),
        scratch_shapes=[pltpu.VMEM((ext, d), BF16), pltpu.VMEM((ext, tn), F32), pltpu.VMEM((ext, tn), F32)],
        compiler_params=_params("parallel", "arbitrary"),
        name="ffn_up",
    )(x, x, x, shift, scale, w, w, conv_w, conv_w)


def _proj_res_kernel(a_ref, w_ref, res_ref, g_ref, o_ref):
    y = jnp.dot(a_ref[...], w_ref[...], preferred_element_type=F32)
    o_ref[...] = res_ref[...] + g_ref[0] * y


def _proj_res_call(a, w_layer, res, gate, *, tm, tiles_per_mod):
    m, k = a.shape
    w = w_layer[0]
    n = w.shape[2]
    tn = n if k * n * 2 <= 8 * 1024 * 1024 else TN
    return pl.pallas_call(
        _proj_res_kernel,
        grid=(m // tm, n // tn),
        in_specs=[
            pl.BlockSpec((tm, k), lambda i, j: (i, 0)),
            _wspec(w_layer, k, tn, 0),
            pl.BlockSpec((tm, tn), lambda i, j: (i, j)),
            pl.BlockSpec((1, 1, tn), lambda i, j: (i // tiles_per_mod, 0, j)),
        ],
        out_specs=pl.BlockSpec((tm, tn), lambda i, j: (i, j)),
        out_shape=jax.ShapeDtypeStruct((m, n), F32),
        compiler_params=_params("parallel", "arbitrary"),
        name="proj_res",
    )(a, w, res, gate)


def _final_norm_kernel(x_ref, g_ref, o_ref):
    x = x_ref[...]
    ms = jnp.mean(x * x, axis=-1, keepdims=True)
    o_ref[...] = (x * lax.rsqrt(ms + EPS)) * g_ref[...]


def _final_norm_call(x, gain, *, tm):
    m, d = x.shape
    return pl.pallas_call(
        _final_norm_kernel,
        grid=(m // tm,),
        in_specs=[pl.BlockSpec((tm, d), lambda i: (i, 0)), pl.BlockSpec((1, d), lambda i: (0, 0))],
        out_specs=pl.BlockSpec((tm, d), lambda i: (i, 0)),
        out_shape=jax.ShapeDtypeStruct((m, d), F32),
        compiler_params=_params("parallel"),
        name="final_norm",
    )(x, gain.reshape(1, d))


def _rope_tables(n_tokens):
    rows = n_tokens // GRID_W
    row = jnp.repeat(jnp.arange(rows, dtype=F32), GRID_W)
    col = jnp.tile(jnp.arange(GRID_W, dtype=F32), rows)
    half = HEAD_DIM // 2
    inv_freq = 1.0 / (ROPE_THETA ** (jnp.arange(0, half, 2, dtype=F32) / half))
    ang_r = row[:, None] * inv_freq
    ang_c = col[:, None] * inv_freq
    zeros = jnp.zeros_like(ang_r)
    cos = jnp.concatenate([jnp.cos(ang_r)] * 2 + [jnp.cos(ang_c)] * 2, axis=-1)
    sin_lo = jnp.concatenate([-jnp.sin(ang_r), zeros, -jnp.sin(ang_c), zeros], axis=-1)
    sin_hi = jnp.concatenate([zeros, jnp.sin(ang_r), zeros, jnp.sin(ang_c)], axis=-1)
    return cos, sin_lo, sin_hi


def kernel(x, c, ctx, c_ctx, w_ada, b_ada, attn_w_qkv, attn_w_o, attn_lambda_q1, attn_lambda_k1,
           attn_lambda_q2, attn_lambda_k2, attn_subln_gain, sconv_w_in, sconv_conv, sconv_w_out,
           ffn_w_up, ffn_conv, ffn_w_down, final_gain):
    b, t, d = x.shape
    lc = ctx.shape[1]
    depth = w_ada.shape[0]
    assert b + 1 <= ADA_ROWS and t % GRID_W == 0
    tm_l = _pick(t, (1024, 512, 256, 128))
    tm_c = _pick(lc, (256, 128))
    lat = dict(tm=tm_l, tiles_per_mod=t // tm_l)
    cx = dict(tm=tm_c, tiles_per_mod=(b * lc) // tm_c)
    lat_seq = dict(tiles_per_seq=t // tm_l, **lat)
    cx_seq = dict(tiles_per_seq=lc // tm_c, **cx)

    cond = jnp.zeros((ADA_ROWS, d), F32).at[:b].set(c).at[b].set(c_ctx)
    mods = _ada_call(cond, w_ada, b_ada).reshape(depth, ADA_ROWS, N_MOD, d)
    tables = [_to_tiles(tbl, tm_l) for tbl in _rope_tables(t)]
    ffn_up_b, ffn_down_b = ffn_w_up.astype(BF16), ffn_w_down.astype(BF16)
    attn_qkv_b, attn_o_b = attn_w_qkv.astype(BF16), attn_w_o.astype(BF16)
    sconv_in_b, sconv_out_b = sconv_w_in.astype(BF16), sconv_w_out.astype(BF16)
    last_attn = max(i for i in range(depth) if i % N_MIXERS == 0)

    xl = _to_tiles(x.reshape(b * t, d), tm_l)
    xc = _to_tiles(ctx.reshape(b * lc, d), tm_c)
    for i in range(depth):
        ctx_update = i < last_attn
        is_attn = i % N_MIXERS == 0
        ctx_read = i <= last_attn and is_attn
        ml = [mods[i, :b, k].reshape(b, 1, d) for k in range(N_MOD)]
        mc = [mods[i, b:b + 1, k].reshape(1, 1, d) for k in range(N_MOD)]
        w_up = (ffn_up_b, i)
        w_down = (ffn_down_b, i)
        if is_attn:
            a = i // N_MIXERS
            lam_init = 0.8 - 0.6 * math.exp(-0.3 * i)
            w_qkv = (attn_qkv_b, a)
            w_o = (attn_o_b, a)
            lams = [p[a].reshape(1, HEAD_DIM) for p in
                    (attn_lambda_q1, attn_lambda_k1, attn_lambda_q2, attn_lambda_k2)]
            gain = attn_subln_gain[a].reshape(1, V_HEAD_DIM)
            qt_l, k_l, vt_l = _qkv_call(xl, ml[0], ml[1], w_qkv, tables, n_seq=b, with_q=True, rope=True, **lat)
            srcs = [(k_l.reshape(b, t, QK_WIDTH), vt_l)]
            if ctx_read:
                outs_c = _qkv_call(xc, mc[0], mc[1], w_qkv, tables, n_seq=b, with_q=ctx_update, rope=False, **cx)
                k_c, vt_c = outs_c[-2].reshape(b, lc, QK_WIDTH), outs_c[-1]
                srcs.append((k_c, vt_c))
            o_l = _attn_call(qt_l, srcs, lams, gain, lam_init=lam_init)
            xl = _proj_res_call(o_l, w_o, xl, ml[2], **lat)
            if ctx_update:
                o_c = _attn_call(outs_c[0], [(k_c, vt_c)], lams, gain, lam_init=lam_init)
                xc = _proj_res_call(o_c, w_o, xc, mc[2], **cx)
        else:
            jx = i // N_MIXERS
            w_in = (sconv_in_b, jx)
            w_out = (sconv_out_b, jx)
            z_l = _sconv_in_call(xl, ml[0], ml[1], w_in, sconv_conv[jx], **lat_seq)
            xl = _proj_res_call(z_l, w_out, xl, ml[2], **lat)
            if ctx_update:
                z_c = _sconv_in_call(xc, mc[0], mc[1], w_in, sconv_conv[jx], **cx_seq)
                xc = _proj_res_call(z_c, w_out, xc, mc[2], **cx)
        u_l = _ffn_up_call(xl, ml[3], ml[4], w_up, ffn_conv[i], **lat_seq)
        xl = _proj_res_call(u_l, w_down, xl, ml[5], **lat)
        if ctx_update:
            u_c = _ffn_up_call(xc, mc[3], mc[4], w_up, ffn_conv[i], **cx_seq)
            xc = _proj_res_call(u_c, w_down, xc, mc[5], **cx)
    return _from_tiles(_final_norm_call(xl, final_gain, tm=tm_l), tm_l).reshape(b, t, d)
```

```python
import functools
import math

import jax
import jax.numpy as jnp
from jax import lax
from jax.experimental import pallas as pl
from jax.experimental.pallas import tpu as pltpu

F32 = jnp.float32
BF16 = jnp.bfloat16

GRID_W = 64
N_MIXERS = 2
DIFF_HEADS = 8
HEAD_DIM = 128
V_HEAD_DIM = 2 * HEAD_DIM
QK_WIDTH = DIFF_HEADS * 2 * HEAD_DIM
V_WIDTH = DIFF_HEADS * V_HEAD_DIM
ROPE_THETA = 10000.0
N_MOD = 6
EPS = 1e-6

LANES = 128
SUBLANES = 8
HALO = 16
V7X_VMEM_LIMIT = 56 * 1024 * 1024
ADA_ROWS = 8
TN = 512
TQ = 512
Q_SCALE = HEAD_DIM ** -0.5 * math.log2(math.e)


def _pick(n, candidates):
    for c in candidates:
        if n % c == 0:
            return c
    return n


def _params(*sem):
    return pltpu.CompilerParams(dimension_semantics=sem, vmem_limit_bytes=V7X_VMEM_LIMIT)


def _norm_mod(x, shift, scale):
    ms = jnp.mean(x * x, axis=-1, keepdims=True)
    return (x * lax.rsqrt(ms + EPS)) * (1.0 + scale) + shift


def _ada_kernel(c_ref, w_ref, b_ref, o_ref):
    c = c_ref[...]
    s = (c * jax.nn.sigmoid(c)).astype(BF16)
    o_ref[0] = jnp.dot(s, w_ref[0].astype(BF16), preferred_element_type=F32) + b_ref[0]


def _ada_call(cond, w_ada, b_ada):
    depth, d, n = w_ada.shape
    tn = _pick(n, (1024, 512, 256, 128))
    return pl.pallas_call(
        _ada_kernel,
        grid=(depth, n // tn),
        in_specs=[
            pl.BlockSpec((ADA_ROWS, d), lambda l, j: (0, 0)),
            pl.BlockSpec((1, d, tn), lambda l, j: (l, 0, j)),
            pl.BlockSpec((1, 1, tn), lambda l, j: (l, 0, j)),
        ],
        out_specs=pl.BlockSpec((1, ADA_ROWS, tn), lambda l, j: (l, 0, j)),
        out_shape=jax.ShapeDtypeStruct((depth, ADA_ROWS, n), F32),
        compiler_params=_params("arbitrary", "arbitrary"),
        name="ada",
    )(cond, w_ada, b_ada.reshape(depth, 1, n))


def _rope(y, cos, sin_lo, sin_hi):
    return y * cos + pltpu.roll(y, LANES - 32, 1) * sin_lo + pltpu.roll(y, 32, 1) * sin_hi


def _qkv_kernel(x_ref, sh_ref, sc_ref, cos_ref, slo_ref, shi_ref, *rest, tn, with_q, rope):
    if with_q:
        wq_ref, wk_ref, wv_ref, qt_ref, k_ref, vt_ref, h_scr = rest
    else:
        wk_ref, wv_ref, k_ref, vt_ref, h_scr = rest

    @pl.when(pl.program_id(1) == 0)
    def _():
        h_scr[...] = _norm_mod(x_ref[...], sh_ref[0], sc_ref[0]).astype(BF16)

    h = h_scr[...]

    def groups(y):
        for g in range(tn // LANES):
            yg = y[:, g * LANES:(g + 1) * LANES]
            if rope:
                yg = _rope(yg, cos_ref[...], slo_ref[...], shi_ref[...])
            yield g, yg

    if with_q:
        tq = qt_ref.shape[2]
        for g, yg in groups(jnp.dot(h, wq_ref[...], preferred_element_type=F32)):
            ygt = (yg * Q_SCALE).T.astype(BF16)
            for blk in range(qt_ref.shape[0]):
                qt_ref[blk, g * LANES:(g + 1) * LANES, :] = ygt[:, blk * tq:(blk + 1) * tq]
    for g, yg in groups(jnp.dot(h, wk_ref[...], preferred_element_type=F32)):
        k_ref[:, g * LANES:(g + 1) * LANES] = yg.astype(BF16)
    vt_ref[...] = jnp.dot(h, wv_ref[...], preferred_element_type=F32).T.astype(BF16)


def _qkv_call(x, shift, scale, w_layer, tables, *, n_seq, tm, tiles_per_mod, with_q, rope):
    m, d = x.shape
    tn = TN
    assert QK_WIDTH == V_WIDTH
    seq = m // n_seq
    tps = seq // tm
    nt = QK_WIDTH // tn
    cos, slo, shi = tables
    pos_tiles = max(cos.shape[0] // tm, 1)
    tbl = pl.BlockSpec((tm, LANES), lambda i, j: (i % pos_tiles, 0))
    mod = pl.BlockSpec((1, 1, d), lambda i, j: (i // tiles_per_mod, 0, 0))
    wspec = lambda off: _wspec(w_layer, d, tn, off)
    w_specs = [wspec(nt), wspec(2 * nt)]
    out_specs = [
        pl.BlockSpec((tm, tn), lambda i, j: (i, j)),
        pl.BlockSpec((None, None, tn, tm), lambda i, j: (i // tps, i % tps, j, 0)),
    ]
    out_shape = [
        jax.ShapeDtypeStruct((m, QK_WIDTH), BF16),
        jax.ShapeDtypeStruct((n_seq, tps, V_WIDTH, tm), BF16),
    ]
    if with_q:
        w_specs.insert(0, wspec(0))
        tq = min(TQ, tm)
        out_specs.insert(0, pl.BlockSpec((None, tm // tq, tn, tq), lambda i, j: (i // tps, i % tps, j, 0)))
        out_shape.insert(0, jax.ShapeDtypeStruct((n_seq, seq // tq, QK_WIDTH, tq), BF16))
    return pl.pallas_call(
        functools.partial(_qkv_kernel, tn=tn, with_q=with_q, rope=rope),
        grid=(m // tm, nt),
        in_specs=[pl.BlockSpec((tm, d), lambda i, j: (i, 0)), mod, mod, tbl, tbl, tbl] + w_specs,
        out_specs=out_specs,
        out_shape=out_shape,
        scratch_shapes=[pltpu.VMEM((tm, d), BF16)],
        compiler_params=_params("parallel", "arbitrary"),
        name="qkv",
    )(x, shift, scale, cos, slo, shi, *([w_layer[0]] * len(w_specs)))


def _scores(k, qt):
    return (jnp.dot(k[:, :HEAD_DIM], qt[:HEAD_DIM, :], preferred_element_type=F32),
            jnp.dot(k[:, HEAD_DIM:], qt[HEAD_DIM:, :], preferred_element_type=F32))


def _colmax(s):
    return jnp.max(s, axis=0, keepdims=True)


def _softmax_update(s, mc, vt, acc, m, l):
    m_new = jnp.maximum(m, mc)
    alpha = jnp.exp2(m - m_new)
    p = jnp.exp2(s - m_new)
    l_new = alpha * l + jnp.sum(p, axis=0, keepdims=True)
    acc[...] = alpha * acc[...] + jnp.dot(vt, p.astype(BF16), preferred_element_type=F32)
    return m_new, l_new


def _attn_update(s, mc, vt, accs, carry):
    m0, l0, m1, l1 = carry
    m0, l0 = _softmax_update(s[0], mc[0], vt, accs[0], m0, l0)
    m1, l1 = _softmax_update(s[1], mc[1], vt, accs[1], m1, l1)
    return m0, l0, m1, l1


def _attn_lambda(lq1_ref, lk1_ref, lq2_ref, lk2_ref, lam_init):
    return (jnp.exp(jnp.sum(lq1_ref[...] * lk1_ref[...], axis=-1, keepdims=True))
            - jnp.exp(jnp.sum(lq2_ref[...] * lk2_ref[...], axis=-1, keepdims=True)) + lam_init)


def _attn_finish(acc0, acc1, l0, l1, lam, gain_ref, lam_init):
    ot = acc0[...] / l0 - lam * (acc1[...] / l1)
    ms = jnp.mean(ot * ot, axis=0, keepdims=True)
    return ((ot * lax.rsqrt(ms + EPS)).T * gain_ref[...] * (1.0 - lam_init)).astype(BF16)


def _attn_init(tq):
    neg = jnp.full((1, tq), -jnp.inf, F32)
    zero = jnp.zeros((1, tq), F32)
    return neg, zero, neg, zero


def _attn_block_kernel(lq1_ref, lk1_ref, lq2_ref, lk2_ref, gain_ref, qt_ref, k_ref, vt_ref, o_ref,
                       acc0, acc1, *, lam_init):
    qt = qt_ref[...]
    acc0[...] = jnp.zeros_like(acc0)
    acc1[...] = jnp.zeros_like(acc1)
    s = _scores(k_ref[...], qt)
    _, l0, _, l1 = _attn_update(s, (_colmax(s[0]), _colmax(s[1])), vt_ref[0], (acc0, acc1),
                                _attn_init(qt.shape[1]))
    lam = _attn_lambda(lq1_ref, lk1_ref, lq2_ref, lk2_ref, lam_init)
    o_ref[...] = _attn_finish(acc0, acc1, l0, l1, lam, gain_ref, lam_init)


def _attn_head_kernel(lq1_ref, lk1_ref, lq2_ref, lk2_ref, gain_ref, q_ref, k_ref, vt_ref, kc_ref, vtc_ref,
                      o_ref, acc0, acc1, fin0, fin1, sa0, sa1, sb0, sb1, *, lam_init):
    nq, _, tq = q_ref.shape
    n_chunks, _, tk = vt_ref.shape
    accs = (acc0, acc1)
    bufs = ((sa0, sa1), (sb0, sb1))
    lam = _attn_lambda(lq1_ref, lk1_ref, lq2_ref, lk2_ref, lam_init)

    def stage(qt, c, buf):
        s = _scores(k_ref[c * tk:(c + 1) * tk, :], qt)
        buf[0][...] = s[0]
        buf[1][...] = s[1]
        return _colmax(s[0]), _colmax(s[1])

    def consume(buf, mc, c, carry):
        return _attn_update((buf[0][...], buf[1][...]), mc, vt_ref[c], accs, carry)

    def store_finished(blk, l0, l1):
        rows = pl.ds(pl.multiple_of(blk * tq, tq), tq)
        o_ref[rows, :] = _attn_finish(fin0, fin1, l0, l1, lam, gain_ref, lam_init)

    def body(qi, state):
        l0_prev, l1_prev, mc = state[0], state[1], state[2:]
        qt = q_ref[qi]
        qt_next = q_ref[jnp.minimum(qi + 1, nq - 1)]
        acc0[...] = jnp.zeros_like(acc0)
        acc1[...] = jnp.zeros_like(acc1)
        carry = _attn_init(tq)
        s_lead = _scores(kc_ref[...], qt)
        mc_next = stage(qt, 1, bufs[1])
        store_finished(jnp.maximum(qi - 1, 0), l0_prev, l1_prev)
        carry = _attn_update(s_lead, (_colmax(s_lead[0]), _colmax(s_lead[1])), vtc_ref[0], accs, carry)
        for c in range(n_chunks):
            mc_cur, cur = (mc, bufs[c % 2]) if c == 0 else (mc_next, bufs[c % 2])
            if 1 <= c < n_chunks - 1:
                mc_after = stage(qt, c + 1, bufs[(c + 1) % 2])
            elif c == n_chunks - 1:
                mc_after = stage(qt_next, 0, bufs[0])
            carry = consume(cur, mc_cur, c, carry)
            if c >= 1:
                mc_next = mc_after
        fin0[...] = acc0[...]
        fin1[...] = acc1[...]
        return (carry[1], carry[3]) + tuple(mc_next)

    fin0[...] = jnp.zeros_like(fin0)
    fin1[...] = jnp.zeros_like(fin1)
    one = jnp.ones((1, tq), F32)
    state = lax.fori_loop(0, nq, body, (one, one) + stage(q_ref[0], 0, bufs[0]))
    store_finished(nq - 1, state[0], state[1])


def _attn_call(qt, srcs, lams, gain, *, lam_init):
    n_seq, nq, _, tq = qt.shape
    hw = V_HEAD_DIM
    const = lambda shape: pl.BlockSpec(shape, lambda *g: (0,) * len(shape))
    small = [const((1, HEAD_DIM))] * 4 + [const((1, hw))]
    acc = pltpu.VMEM((hw, tq), F32)
    out_shape = jax.ShapeDtypeStruct((n_seq * nq * tq, V_WIDTH), BF16)
    if len(srcs) == 1:
        (k, vt), = srcs
        assert vt.shape[1] == 1
        return pl.pallas_call(
            functools.partial(_attn_block_kernel, lam_init=lam_init),
            grid=(n_seq, DIFF_HEADS, nq),
            in_specs=small + [
                pl.BlockSpec((None, None, hw, tq), lambda b, h, i: (b, i, h, 0)),
                pl.BlockSpec((None, k.shape[1], hw), lambda b, h, i: (b, 0, h)),
                pl.BlockSpec((None, 1, hw, vt.shape[3]), lambda b, h, i: (b, 0, h, 0)),
            ],
            out_specs=pl.BlockSpec((tq, hw), lambda b, h, i: (b * nq + i, h)),
            out_shape=out_shape,
            scratch_shapes=[acc, acc],
            compiler_params=_params("parallel", "parallel", "arbitrary"),
            name="diff_attn_block",
        )(*lams, gain, qt, k, vt)
    (k, vt), (kc, vtc) = srcs
    n_chunks, tk = vt.shape[1], vt.shape[3]
    assert n_chunks % 2 == 0 and vtc.shape[1] == 1
    s_scr = pltpu.VMEM((tk, tq), F32)
    per_head = lambda shape: pl.BlockSpec((None,) + shape, lambda b, h: (b, 0, h) + (0,) * (len(shape) - 2))
    return pl.pallas_call(
        functools.partial(_attn_head_kernel, lam_init=lam_init),
        grid=(n_seq, DIFF_HEADS),
        in_specs=small + [
            per_head((nq, hw, tq)),
            pl.BlockSpec((None, k.shape[1], hw), lambda b, h: (b, 0, h)),
            per_head((n_chunks, hw, tk)),
            pl.BlockSpec((None, kc.shape[1], hw), lambda b, h: (b, 0, h)),
            per_head((1, hw, vtc.shape[3])),
        ],
        out_specs=pl.BlockSpec((nq * tq, hw), lambda b, h: (b, h)),
        out_shape=out_shape,
        scratch_shapes=[acc, acc, acc, acc, s_scr, s_scr, s_scr, s_scr],
        compiler_params=_params("parallel", "arbitrary"),
        name="diff_attn",
    )(*lams, gain, qt, k, vt, kc, vtc)


def _to_tiles(x, tm):
    m, d = x.shape
    return x.reshape(m // tm, SUBLANES, tm // SUBLANES, d).swapaxes(1, 2).reshape(m, d)


def _from_tiles(x, tm):
    m, d = x.shape
    return x.reshape(m // tm, tm // SUBLANES, SUBLANES, d).swapaxes(1, 2).reshape(m, d)


def _fill_h(h_scr, x_ref, xp_ref, xn_ref, sh, sc, first, last, tm):
    h = _norm_mod(x_ref[...], sh, sc)
    prev_tok = jnp.where(first, 0.0, _norm_mod(xp_ref[...], sh, sc)[HALO - 1:HALO, :])
    next_tok = jnp.where(last, 0.0, _norm_mod(xn_ref[...], sh, sc)[0:1, :])
    row = lax.broadcasted_iota(jnp.int32, (SUBLANES, h.shape[1]), 0)
    up = jnp.where(row == 0, prev_tok, pltpu.roll(h[tm - SUBLANES:, :], 1, 0))
    down = jnp.where(row == SUBLANES - 1, next_tok, pltpu.roll(h[:SUBLANES, :], SUBLANES - 1, 0))
    zeros = jnp.zeros_like(up)
    h_scr[0:HALO, :] = jnp.concatenate([zeros, up], axis=0).astype(BF16)
    h_scr[HALO:HALO + tm, :] = h.astype(BF16)
    h_scr[HALO + tm:, :] = jnp.concatenate([down, zeros], axis=0).astype(BF16)


def _conv3(y_scr, w_ref, tm):
    return (y_scr[pl.ds(HALO - SUBLANES, tm), :] * w_ref[0:1, :]
            + y_scr[pl.ds(HALO, tm), :] * w_ref[1:2, :]
            + y_scr[pl.ds(HALO + SUBLANES, tm), :] * w_ref[2:3, :])


def _halo_specs(tm, d, m):
    r = tm // HALO
    last_blk = m // HALO - 1
    return [
        pl.BlockSpec((tm, d), lambda i, j: (i, 0)),
        pl.BlockSpec((HALO, d), lambda i, j: (jnp.maximum(i * r - 1, 0), 0)),
        pl.BlockSpec((HALO, d), lambda i, j: (jnp.minimum((i + 1) * r, last_blk), 0)),
    ]


def _conv_prologue(x_ref, xp_ref, xn_ref, sh_ref, sc_ref, h_scr, *, tm, tiles_per_seq):
    @pl.when(pl.program_id(1) == 0)
    def _():
        t = pl.program_id(0) % tiles_per_seq
        _fill_h(h_scr, x_ref, xp_ref, xn_ref, sh_ref[0], sc_ref[0], t == 0, t == tiles_per_seq - 1, tm)


def _wspec(w_layer, d, tn, off):
    layer = w_layer[1]
    return pl.BlockSpec((None, d, tn), lambda i, j: (layer, 0, j + off))


def _sconv_in_kernel(x_ref, xp_ref, xn_ref, sh_ref, sc_ref, wb_ref, wc_ref, wx_ref, cw_ref, o_ref,
                     h_scr, p_scr, *, tm, tiles_per_seq):
    _conv_prologue(x_ref, xp_ref, xn_ref, sh_ref, sc_ref, h_scr, tm=tm, tiles_per_seq=tiles_per_seq)
    h = h_scr[...]
    p_scr[...] = (jnp.dot(h, wc_ref[...], preferred_element_type=F32)
                  * jnp.dot(h, wx_ref[...], preferred_element_type=F32))
    gb = jnp.dot(h_scr[HALO:HALO + tm, :], wb_ref[...], preferred_element_type=F32)
    o_ref[...] = (gb * _conv3(p_scr, cw_ref, tm)).astype(o_ref.dtype)


def _sconv_in_call(x, shift, scale, w_layer, conv_w, *, tm, tiles_per_mod, tiles_per_seq):
    m, d = x.shape
    w = w_layer[0]
    tn = TN
    nt = w.shape[2] // 3 // tn
    mod = pl.BlockSpec((1, 1, d), lambda i, j: (i // tiles_per_mod, 0, 0))
    ext = tm + 2 * HALO
    return pl.pallas_call(
        functools.partial(_sconv_in_kernel, tm=tm, tiles_per_seq=tiles_per_seq),
        grid=(m // tm, nt),
        in_specs=_halo_specs(tm, d, m) + [
            mod, mod, _wspec(w_layer, d, tn, 0), _wspec(w_layer, d, tn, nt), _wspec(w_layer, d, tn, 2 * nt),
            pl.BlockSpec((3, tn), lambda i, j: (0, j)),
        ],
        out_specs=pl.BlockSpec((tm, tn), lambda i, j: (i, j)),
        out_shape=jax.ShapeDtypeStruct((m, nt * tn), BF16),
        scratch_shapes=[pltpu.VMEM((ext, d), BF16), pltpu.VMEM((ext, tn), F32)],
        compiler_params=_params("parallel", "arbitrary"),
        name="sconv_in",
    )(x, x, x, shift, scale, w, w, w, conv_w)


def _ffn_up_kernel(x_ref, xp_ref, xn_ref, sh_ref, sc_ref, wg_ref, wv_ref, cg_ref, cv_ref, o_ref,
                   h_scr, yg_scr, yv_scr, *, tm, tiles_per_seq):
    _conv_prologue(x_ref, xp_ref, xn_ref, sh_ref, sc_ref, h_scr, tm=tm, tiles_per_seq=tiles_per_seq)
    h = h_scr[...]
    yg_scr[...] = jnp.dot(h, wg_ref[...], preferred_element_type=F32)
    yv_scr[...] = jnp.dot(h, wv_ref[...], preferred_element_type=F32)
    ug = _conv3(yg_scr, cg_ref, tm)
    uv = _conv3(yv_scr, cv_ref, tm)
    o_ref[...] = (ug * jax.nn.sigmoid(ug) * uv).astype(o_ref.dtype)


def _ffn_up_call(x, shift, scale, w_layer, conv_w, *, tm, tiles_per_mod, tiles_per_seq):
    m, d = x.shape
    w = w_layer[0]
    tn = TN
    nt = w.shape[2] // 2 // tn
    mod = pl.BlockSpec((1, 1, d), lambda i, j: (i // tiles_per_mod, 0, 0))
    cspec = lambda off: pl.BlockSpec((3, tn), lambda i, j: (0, j + off))
    ext = tm + 2 * HALO
    return pl.pallas_call(
        functools.partial(_ffn_up_kernel, tm=tm, tiles_per_seq=tiles_per_seq),
        grid=(m // tm, nt),
        in_specs=_halo_specs(tm, d, m) + [
            mod, mod, _wspec(w_layer, d, tn, 0), _wspec(w_layer, d, tn, nt), cspec(0), cspec(nt)],
        out_specs=pl.BlockSpec((tm, tn), lambda i, j: (i, j)),
        out_shape=jax.ShapeDtypeStruct((m, nt * tn), BF16),
        scratch_shapes=[pltpu.VMEM((ext, d), BF16), pltpu.VMEM((ext, tn), F32), pltpu.VMEM((ext, tn), F32)],
        compiler_params=_params("parallel", "arbitrary"),
        name="ffn_up",
    )(x, x, x, shift, scale, w, w, conv_w, conv_w)


def _proj_res_kernel(a_ref, w_ref, res_ref, g_ref, o_ref):
    y = jnp.dot(a_ref[...], w_ref[...], preferred_element_type=F32)
    o_ref[...] = res_ref[...] + g_ref[0] * y


def _proj_res_call(a, w_layer, res, gate, *, tm, tiles_per_mod):
    m, k = a.shape
    w = w_layer[0]
    n = w.shape[2]
    tn = n if k * n * 2 <= 8 * 1024 * 1024 else TN
    return pl.pallas_call(
        _proj_res_kernel,
        grid=(m // tm, n // tn),
        in_specs=[
            pl.BlockSpec((tm, k), lambda i, j: (i, 0)),
            _wspec(w_layer, k, tn, 0),
            pl.BlockSpec((tm, tn), lambda i, j: (i, j)),
            pl.BlockSpec((1, 1, tn), lambda i, j: (i // tiles_per_mod, 0, j)),
        ],
        out_specs=pl.BlockSpec((tm, tn), lambda i, j: (i, j)),
        out_shape=jax.ShapeDtypeStruct((m, n), F32),
        compiler_params=_params("parallel", "arbitrary"),
        name="proj_res",
    )(a, w, res, gate)


def _final_norm_kernel(x_ref, g_ref, o_ref):
    x = x_ref[...]
    ms = jnp.mean(x * x, axis=-1, keepdims=True)
    o_ref[...] = (x * lax.rsqrt(ms + EPS)) * g_ref[...]


def _final_norm_call(x, gain, *, tm):
    m, d = x.shape
    return pl.pallas_call(
        _final_norm_kernel,
        grid=(m // tm,),
        in_specs=[pl.BlockSpec((tm, d), lambda i: (i, 0)), pl.BlockSpec((1, d), lambda i: (0, 0))],
        out_specs=pl.BlockSpec((tm, d), lambda i: (i, 0)),
        out_shape=jax.ShapeDtypeStruct((m, d), F32),
        compiler_params=_params("parallel"),
        name="final_norm",
    )(x, gain.reshape(1, d))


def _rope_tables(n_tokens):
    rows = n_tokens // GRID_W
    row = jnp.repeat(jnp.arange(rows, dtype=F32), GRID_W)
    col = jnp.tile(jnp.arange(GRID_W, dtype=F32), rows)
    half = HEAD_DIM // 2
    inv_freq = 1.0 / (ROPE_THETA ** (jnp.arange(0, half, 2, dtype=F32) / half))
    ang_r = row[:, None] * inv_freq
    ang_c = col[:, None] * inv_freq
    zeros = jnp.zeros_like(ang_r)
    cos = jnp.concatenate([jnp.cos(ang_r)] * 2 + [jnp.cos(ang_c)] * 2, axis=-1)
    sin_lo = jnp.concatenate([-jnp.sin(ang_r), zeros, -jnp.sin(ang_c), zeros], axis=-1)
    sin_hi = jnp.concatenate([zeros, jnp.sin(ang_r), zeros, jnp.sin(ang_c)], axis=-1)
    return cos, sin_lo, sin_hi


def kernel(x, c, ctx, c_ctx, w_ada, b_ada, attn_w_qkv, attn_w_o, attn_lambda_q1, attn_lambda_k1,
           attn_lambda_q2, attn_lambda_k2, attn_subln_gain, sconv_w_in, sconv_conv, sconv_w_out,
           ffn_w_up, ffn_conv, ffn_w_down, final_gain):
    b, t, d = x.shape
    lc = ctx.shape[1]
    depth = w_ada.shape[0]
    assert b + 1 <= ADA_ROWS and t % GRID_W == 0
    tm_l = _pick(t, (1024, 512, 256, 128))
    tm_c = _pick(lc, (256, 128))
    lat = dict(tm=tm_l, tiles_per_mod=t // tm_l)
    cx = dict(tm=tm_c, tiles_per_mod=(b * lc) // tm_c)
    lat_seq = dict(tiles_per_seq=t // tm_l, **lat)
    cx_seq = dict(tiles_per_seq=lc // tm_c, **cx)

    cond = jnp.zeros((ADA_ROWS, d), F32).at[:b].set(c).at[b].set(c_ctx)
    mods = _ada_call(cond, w_ada, b_ada).reshape(depth, ADA_ROWS, N_MOD, d)
    tables = [_to_tiles(tbl, tm_l) for tbl in _rope_tables(t)]
    ffn_up_b, ffn_down_b = ffn_w_up.astype(BF16), ffn_w_down.astype(BF16)
    attn_qkv_b, attn_o_b = attn_w_qkv.astype(BF16), attn_w_o.astype(BF16)
    sconv_in_b, sconv_out_b = sconv_w_in.astype(BF16), sconv_w_out.astype(BF16)
    last_attn = max(i for i in range(depth) if i % N_MIXERS == 0)

    xl = _to_tiles(x.reshape(b * t, d), tm_l)
    xc = _to_tiles(ctx.reshape(b * lc, d), tm_c)
    for i in range(depth):
        ctx_update = i < last_attn
        is_attn = i % N_MIXERS == 0
        ctx_read = i <= last_attn and is_attn
        ml = [mods[i, :b, k].reshape(b, 1, d) for k in range(N_MOD)]
        mc = [mods[i, b:b + 1, k].reshape(1, 1, d) for k in range(N_MOD)]
        w_up = (ffn_up_b, i)
        w_down = (ffn_down_b, i)
        if is_attn:
            a = i // N_MIXERS
            lam_init = 0.8 - 0.6 * math.exp(-0.3 * i)
            w_qkv = (attn_qkv_b, a)
            w_o = (attn_o_b, a)
            lams = [p[a].reshape(1, HEAD_DIM) for p in
                    (attn_lambda_q1, attn_lambda_k1, attn_lambda_q2, attn_lambda_k2)]
            gain = attn_subln_gain[a].reshape(1, V_HEAD_DIM)
            qt_l, k_l, vt_l = _qkv_call(xl, ml[0], ml[1], w_qkv, tables, n_seq=b, with_q=True, rope=True, **lat)
            srcs = [(k_l.reshape(b, t, QK_WIDTH), vt_l)]
            if ctx_read:
                outs_c = _qkv_call(xc, mc[0], mc[1], w_qkv, tables, n_seq=b, with_q=ctx_update, rope=False, **cx)
                k_c, vt_c = outs_c[-2].reshape(b, lc, QK_WIDTH), outs_c[-1]
                srcs.append((k_c, vt_c))
            o_l = _attn_call(qt_l, srcs, lams, gain, lam_init=lam_init)
            xl = _proj_res_call(o_l, w_o, xl, ml[2], **lat)
            if ctx_update:
                o_c = _attn_call(outs_c[0], [(k_c, vt_c)], lams, gain, lam_init=lam_init)
                xc = _proj_res_call(o_c, w_o, xc, mc[2], **cx)
        else:
            jx = i // N_MIXERS
            w_in = (sconv_in_b, jx)
            w_out = (sconv_out_b, jx)
            z_l = _sconv_in_call(xl, ml[0], ml[1], w_in, sconv_conv[jx], **lat_seq)
            xl = _proj_res_call(z_l, w_out, xl, ml[2], **lat)
            if ctx_update:
                z_c = _sconv_in_call(xc, mc[0], mc[1], w_in, sconv_conv[jx], **cx_seq)
                xc = _proj_res_call(z_c, w_out, xc, mc[2], **cx)
        u_l = _ffn_up_call(xl, ml[3], ml[4], w_up, ffn_conv[i], **lat_seq)
        xl = _proj_res_call(u_l, w_down, xl, ml[5], **lat)
        if ctx_update:
            u_c = _ffn_up_call(xc, mc[3], mc[4], w_up, ffn_conv[i], **cx_seq)
            xc = _proj_res_call(u_c, w_down, xc, mc[5], **cx)
    return _from_tiles(_final_norm_call(xl, final_gain, tm=tm_l), tm_l).reshape(b, t, d)
```

```python
import functools
import math

import jax
import jax.numpy as jnp
from jax import lax
from jax.experimental import pallas as pl
from jax.experimental.pallas import tpu as pltpu

F32 = jnp.float32
BF16 = jnp.bfloat16

GRID_W = 64
N_MIXERS = 2
DIFF_HEADS = 8
HEAD_DIM = 128
V_HEAD_DIM = 2 * HEAD_DIM
QK_WIDTH = DIFF_HEADS * 2 * HEAD_DIM
V_WIDTH = DIFF_HEADS * V_HEAD_DIM
ROPE_THETA = 10000.0
N_MOD = 6
EPS = 1e-6

LANES = 128
SUBLANES = 8
HALO = 16
V7X_VMEM_LIMIT = 56 * 1024 * 1024
ADA_ROWS = 8
TN = 512
TQ = 512
STRIP_ROWS = 256
Q_SCALE = HEAD_DIM ** -0.5 * math.log2(math.e)


def _pick(n, candidates):
    for c in candidates:
        if n % c == 0:
            return c
    return n


def _params(*sem):
    return pltpu.CompilerParams(dimension_semantics=sem, vmem_limit_bytes=V7X_VMEM_LIMIT)


def _norm_mod(x, shift, scale):
    ms = jnp.mean(x * x, axis=-1, keepdims=True)
    return (x * lax.rsqrt(ms + EPS)) * (1.0 + scale) + shift


def _ada_kernel(c_ref, w_ref, b_ref, o_ref):
    c = c_ref[...]
    s = (c * jax.nn.sigmoid(c)).astype(BF16)
    o_ref[0] = jnp.dot(s, w_ref[0].astype(BF16), preferred_element_type=F32) + b_ref[0]


def _ada_call(cond, w_ada, b_ada):
    depth, d, n = w_ada.shape
    tn = _pick(n, (1024, 512, 256, 128))
    return pl.pallas_call(
        _ada_kernel,
        grid=(depth, n // tn),
        in_specs=[
            pl.BlockSpec((ADA_ROWS, d), lambda l, j: (0, 0)),
            pl.BlockSpec((1, d, tn), lambda l, j: (l, 0, j)),
            pl.BlockSpec((1, 1, tn), lambda l, j: (l, 0, j)),
        ],
        out_specs=pl.BlockSpec((1, ADA_ROWS, tn), lambda l, j: (l, 0, j)),
        out_shape=jax.ShapeDtypeStruct((depth, ADA_ROWS, n), F32),
        compiler_params=_params("arbitrary", "arbitrary"),
        name="ada",
    )(cond, w_ada, b_ada.reshape(depth, 1, n))


def _rope(y, cos, sin_lo, sin_hi):
    return y * cos + pltpu.roll(y, LANES - 32, 1) * sin_lo + pltpu.roll(y, 32, 1) * sin_hi


def _qkv_kernel(x_ref, sh_ref, sc_ref, cos_ref, slo_ref, shi_ref, *rest, tn, with_q, rope):
    if with_q:
        wq_ref, wk_ref, wv_ref, qt_ref, k_ref, vt_ref, h_scr = rest
    else:
        wk_ref, wv_ref, k_ref, vt_ref, h_scr = rest

    @pl.when(pl.program_id(1) == 0)
    def _():
        h_scr[...] = _norm_mod(x_ref[...], sh_ref[0], sc_ref[0]).astype(BF16)

    h = h_scr[...]

    def groups(y):
        for g in range(tn // LANES):
            yg = y[:, g * LANES:(g + 1) * LANES]
            if rope:
                yg = _rope(yg, cos_ref[...], slo_ref[...], shi_ref[...])
            yield g, yg

    yq = jnp.dot(h, wq_ref[...], preferred_element_type=F32) if with_q else None
    yk = jnp.dot(h, wk_ref[...], preferred_element_type=F32)
    yv = jnp.dot(h, wv_ref[...], preferred_element_type=F32)
    if with_q:
        tq = qt_ref.shape[2]
        for g, yg in groups(yq):
            ygt = (yg * Q_SCALE).T.astype(BF16)
            for blk in range(qt_ref.shape[0]):
                qt_ref[blk, g * LANES:(g + 1) * LANES, :] = ygt[:, blk * tq:(blk + 1) * tq]
    for g, yg in groups(yk):
        k_ref[:, g * LANES:(g + 1) * LANES] = yg.astype(BF16)
    vt_ref[...] = yv.T.astype(BF16)


def _qkv_call(x, shift, scale, w_layer, tables, *, n_seq, tm, tiles_per_mod, with_q, rope):
    m, d = x.shape
    tn = TN
    assert QK_WIDTH == V_WIDTH
    seq = m // n_seq
    tps = seq // tm
    nt = QK_WIDTH // tn
    cos, slo, shi = tables
    pos_tiles = max(cos.shape[0] // tm, 1)
    tbl = pl.BlockSpec((tm, LANES), lambda i, j: (i % pos_tiles, 0))
    mod = pl.BlockSpec((1, 1, d), lambda i, j: (i // tiles_per_mod, 0, 0))
    wspec = lambda off: _wspec(w_layer, d, tn, off)
    w_specs = [wspec(nt), wspec(2 * nt)]
    out_specs = [
        pl.BlockSpec((tm, tn), lambda i, j: (i, j)),
        pl.BlockSpec((None, None, tn, tm), lambda i, j: (i // tps, i % tps, j, 0)),
    ]
    out_shape = [
        jax.ShapeDtypeStruct((m, QK_WIDTH), BF16),
        jax.ShapeDtypeStruct((n_seq, tps, V_WIDTH, tm), BF16),
    ]
    if with_q:
        w_specs.insert(0, wspec(0))
        tq = min(TQ, tm)
        out_specs.insert(0, pl.BlockSpec((None, tm // tq, tn, tq), lambda i, j: (i // tps, i % tps, j, 0)))
        out_shape.insert(0, jax.ShapeDtypeStruct((n_seq, seq // tq, QK_WIDTH, tq), BF16))
    return pl.pallas_call(
        functools.partial(_qkv_kernel, tn=tn, with_q=with_q, rope=rope),
        grid=(m // tm, nt),
        in_specs=[pl.BlockSpec((tm, d), lambda i, j: (i, 0)), mod, mod, tbl, tbl, tbl] + w_specs,
        out_specs=out_specs,
        out_shape=out_shape,
        scratch_shapes=[pltpu.VMEM((tm, d), BF16)],
        compiler_params=_params("parallel", "arbitrary"),
        name="qkv",
    )(x, shift, scale, cos, slo, shi, *([w_layer[0]] * len(w_specs)))


def _scores(k, qt):
    return (jnp.dot(k[:, :HEAD_DIM], qt[:HEAD_DIM, :], preferred_element_type=F32),
            jnp.dot(k[:, HEAD_DIM:], qt[HEAD_DIM:, :], preferred_element_type=F32))


def _colmax(s):
    return jnp.max(s, axis=0, keepdims=True)


def _softmax_update(s, mc, vt, acc, m, l):
    m_new = jnp.maximum(m, mc)
    alpha = jnp.exp2(m - m_new)
    p = jnp.exp2(s - m_new)
    l_new = alpha * l + jnp.sum(p, axis=0, keepdims=True)
    acc[...] = alpha * acc[...] + jnp.dot(vt, p.astype(BF16), preferred_element_type=F32)
    return m_new, l_new


def _attn_update(s, mc, vt, accs, carry):
    m0, l0, m1, l1 = carry
    m0, l0 = _softmax_update(s[0], mc[0], vt, accs[0], m0, l0)
    m1, l1 = _softmax_update(s[1], mc[1], vt, accs[1], m1, l1)
    return m0, l0, m1, l1


def _attn_lambda(lq1_ref, lk1_ref, lq2_ref, lk2_ref, lam_init):
    return (jnp.exp(jnp.sum(lq1_ref[...] * lk1_ref[...], axis=-1, keepdims=True))
            - jnp.exp(jnp.sum(lq2_ref[...] * lk2_ref[...], axis=-1, keepdims=True)) + lam_init)


def _attn_finish(acc0, acc1, l0, l1, lam, gain_ref, lam_init):
    ot = acc0[...] / l0 - lam * (acc1[...] / l1)
    ms = jnp.mean(ot * ot, axis=0, keepdims=True)
    return ((ot * lax.rsqrt(ms + EPS)).T * gain_ref[...] * (1.0 - lam_init)).astype(BF16)


def _attn_init(tq):
    neg = jnp.full((1, tq), -jnp.inf, F32)
    zero = jnp.zeros((1, tq), F32)
    return neg, zero, neg, zero


def _attn_block_kernel(lq1_ref, lk1_ref, lq2_ref, lk2_ref, gain_ref, qt_ref, k_ref, vt_ref, o_ref,
                       acc0, acc1, *, lam_init):
    qt = qt_ref[...]
    acc0[...] = jnp.zeros_like(acc0)
    acc1[...] = jnp.zeros_like(acc1)
    s = _scores(k_ref[...], qt)
    _, l0, _, l1 = _attn_update(s, (_colmax(s[0]), _colmax(s[1])), vt_ref[0], (acc0, acc1),
                                _attn_init(qt.shape[1]))
    lam = _attn_lambda(lq1_ref, lk1_ref, lq2_ref, lk2_ref, lam_init)
    o_ref[...] = _attn_finish(acc0, acc1, l0, l1, lam, gain_ref, lam_init)


def _attn_head_kernel(lq1_ref, lk1_ref, lq2_ref, lk2_ref, gain_ref, q_ref, k_ref, vt_ref, kc_ref, vtc_ref,
                      o_ref, acc0, acc1, fin0, fin1, sa0, sa1, sb0, sb1, *, lam_init):
    nq, _, tq = q_ref.shape
    n_chunks, _, tk = vt_ref.shape
    accs = (acc0, acc1)
    bufs = ((sa0, sa1), (sb0, sb1))
    lam = _attn_lambda(lq1_ref, lk1_ref, lq2_ref, lk2_ref, lam_init)

    def stage(qt, c, buf):
        s = _scores(k_ref[c * tk:(c + 1) * tk, :], qt)
        buf[0][...] = s[0]
        buf[1][...] = s[1]
        return _colmax(s[0]), _colmax(s[1])

    def consume(buf, mc, c, carry):
        return _attn_update((buf[0][...], buf[1][...]), mc, vt_ref[c], accs, carry)

    def store_finished(blk, l0, l1):
        rows = pl.ds(pl.multiple_of(blk * tq, tq), tq)
        o_ref[rows, :] = _attn_finish(fin0, fin1, l0, l1, lam, gain_ref, lam_init)

    def body(qi, state):
        l0_prev, l1_prev, mc = state[0], state[1], state[2:]
        qt = q_ref[qi]
        qt_next = q_ref[jnp.minimum(qi + 1, nq - 1)]
        acc0[...] = jnp.zeros_like(acc0)
        acc1[...] = jnp.zeros_like(acc1)
        carry = _attn_init(tq)
        s_lead = _scores(kc_ref[...], qt)
        mc_next = stage(qt, 1, bufs[1])
        store_finished(jnp.maximum(qi - 1, 0), l0_prev, l1_prev)
        carry = _attn_update(s_lead, (_colmax(s_lead[0]), _colmax(s_lead[1])), vtc_ref[0], accs, carry)
        for c in range(n_chunks):
            mc_cur, cur = (mc, bufs[c % 2]) if c == 0 else (mc_next, bufs[c % 2])
            if 1 <= c < n_chunks - 1:
                mc_after = stage(qt, c + 1, bufs[(c + 1) % 2])
            elif c == n_chunks - 1:
                mc_after = stage(qt_next, 0, bufs[0])
            carry = consume(cur, mc_cur, c, carry)
            if c >= 1:
                mc_next = mc_after
        fin0[...] = acc0[...]
        fin1[...] = acc1[...]
        return (carry[1], carry[3]) + tuple(mc_next)

    fin0[...] = jnp.zeros_like(fin0)
    fin1[...] = jnp.zeros_like(fin1)
    one = jnp.ones((1, tq), F32)
    state = lax.fori_loop(0, nq, body, (one, one) + stage(q_ref[0], 0, bufs[0]))
    store_finished(nq - 1, state[0], state[1])


def _attn_call(qt, srcs, lams, gain, *, lam_init):
    n_seq, nq, _, tq = qt.shape
    hw = V_HEAD_DIM
    const = lambda shape: pl.BlockSpec(shape, lambda *g: (0,) * len(shape))
    small = [const((1, HEAD_DIM))] * 4 + [const((1, hw))]
    acc = pltpu.VMEM((hw, tq), F32)
    out_shape = jax.ShapeDtypeStruct((n_seq * nq * tq, V_WIDTH), BF16)
    if len(srcs) == 1:
        (k, vt), = srcs
        assert vt.shape[1] == 1
        return pl.pallas_call(
            functools.partial(_attn_block_kernel, lam_init=lam_init),
            grid=(n_seq, DIFF_HEADS, nq),
            in_specs=small + [
                pl.BlockSpec((None, None, hw, tq), lambda b, h, i: (b, i, h, 0)),
                pl.BlockSpec((None, k.shape[1], hw), lambda b, h, i: (b, 0, h)),
                pl.BlockSpec((None, 1, hw, vt.shape[3]), lambda b, h, i: (b, 0, h, 0)),
            ],
            out_specs=pl.BlockSpec((tq, hw), lambda b, h, i: (b * nq + i, h)),
            out_shape=out_shape,
            scratch_shapes=[acc, acc],
            compiler_params=_params("parallel", "parallel", "arbitrary"),
            name="diff_attn_block",
        )(*lams, gain, qt, k, vt)
    (k, vt), (kc, vtc) = srcs
    n_chunks, tk = vt.shape[1], vt.shape[3]
    assert n_chunks % 2 == 0 and vtc.shape[1] == 1
    s_scr = pltpu.VMEM((tk, tq), F32)
    per_head = lambda shape: pl.BlockSpec((None,) + shape, lambda b, h: (b, 0, h) + (0,) * (len(shape) - 2))
    return pl.pallas_call(
        functools.partial(_attn_head_kernel, lam_init=lam_init),
        grid=(n_seq, DIFF_HEADS),
        in_specs=small + [
            per_head((nq, hw, tq)),
            pl.BlockSpec((None, k.shape[1], hw), lambda b, h: (b, 0, h)),
            per_head((n_chunks, hw, tk)),
            pl.BlockSpec((None, kc.shape[1], hw), lambda b, h: (b, 0, h)),
            per_head((1, hw, vtc.shape[3])),
        ],
        out_specs=pl.BlockSpec((nq * tq, hw), lambda b, h: (b, h)),
        out_shape=out_shape,
        scratch_shapes=[acc, acc, acc, acc, s_scr, s_scr, s_scr, s_scr],
        compiler_params=_params("parallel", "arbitrary"),
        name="diff_attn",
    )(*lams, gain, qt, k, vt, kc, vtc)


def _to_tiles(x, tm):
    m, d = x.shape
    return x.reshape(m // tm, SUBLANES, tm // SUBLANES, d).swapaxes(1, 2).reshape(m, d)


def _from_tiles(x, tm):
    m, d = x.shape
    return x.reshape(m // tm, tm // SUBLANES, SUBLANES, d).swapaxes(1, 2).reshape(m, d)


def _fill_h(h_scr, x_ref, xp_ref, xn_ref, sh, sc, first, last, tm):
    h = _norm_mod(x_ref[...], sh, sc)
    prev_tok = jnp.where(first, 0.0, _norm_mod(xp_ref[...], sh, sc)[HALO - 1:HALO, :])
    next_tok = jnp.where(last, 0.0, _norm_mod(xn_ref[...], sh, sc)[0:1, :])
    row = lax.broadcasted_iota(jnp.int32, (SUBLANES, h.shape[1]), 0)
    up = jnp.where(row == 0, prev_tok, pltpu.roll(h[tm - SUBLANES:, :], 1, 0))
    down = jnp.where(row == SUBLANES - 1, next_tok, pltpu.roll(h[:SUBLANES, :], SUBLANES - 1, 0))
    zeros = jnp.zeros_like(up)
    h_scr[0:HALO, :] = jnp.concatenate([zeros, up], axis=0).astype(BF16)
    h_scr[HALO:HALO + tm, :] = h.astype(BF16)
    h_scr[HALO + tm:, :] = jnp.concatenate([down, zeros], axis=0).astype(BF16)


def _conv3(y_scr, w_ref, lo, hi):
    n = hi - lo
    return (y_scr[pl.ds(HALO - SUBLANES + lo, n), :] * w_ref[0:1, :]
            + y_scr[pl.ds(HALO + lo, n), :] * w_ref[1:2, :]
            + y_scr[pl.ds(HALO + SUBLANES + lo, n), :] * w_ref[2:3, :])


def _strips(tm):
    n = max(tm // STRIP_ROWS, 1)
    inner = tm // n
    edges = [0] + [HALO + inner * (k + 1) for k in range(n - 1)] + [tm + 2 * HALO]
    out_edges = [0] + [e - 2 * HALO for e in edges[1:-1]] + [tm]
    return list(zip(edges[:-1], edges[1:])), list(zip(out_edges[:-1], out_edges[1:]))


def _halo_specs(tm, d, m):
    r = tm // HALO
    last_blk = m // HALO - 1
    return [
        pl.BlockSpec((tm, d), lambda i, j: (i, 0)),
        pl.BlockSpec((HALO, d), lambda i, j: (jnp.maximum(i * r - 1, 0), 0)),
        pl.BlockSpec((HALO, d), lambda i, j: (jnp.minimum((i + 1) * r, last_blk), 0)),
    ]


def _conv_prologue(x_ref, xp_ref, xn_ref, sh_ref, sc_ref, h_scr, *, tm, tiles_per_seq):
    @pl.when(pl.program_id(1) == 0)
    def _():
        t = pl.program_id(0) % tiles_per_seq
        _fill_h(h_scr, x_ref, xp_ref, xn_ref, sh_ref[0], sc_ref[0], t == 0, t == tiles_per_seq - 1, tm)


def _wspec(w_layer, d, tn, off):
    layer = w_layer[1]
    return pl.BlockSpec((None, d, tn), lambda i, j: (layer, 0, j + off))


def _sconv_in_kernel(x_ref, xp_ref, xn_ref, sh_ref, sc_ref, wb_ref, wc_ref, wx_ref, cw_ref, o_ref,
                     h_scr, p_scr, *, tm, tiles_per_seq):
    _conv_prologue(x_ref, xp_ref, xn_ref, sh_ref, sc_ref, h_scr, tm=tm, tiles_per_seq=tiles_per_seq)
    h = h_scr[...]
    p_scr[...] = (jnp.dot(h, wc_ref[...], preferred_element_type=F32)
                  * jnp.dot(h, wx_ref[...], preferred_element_type=F32))
    gb = jnp.dot(h_scr[HALO:HALO + tm, :], wb_ref[...], preferred_element_type=F32)
    o_ref[...] = (gb * _conv3(p_scr, cw_ref, 0, tm)).astype(o_ref.dtype)


def _sconv_in_call(x, shift, scale, w_layer, conv_w, *, tm, tiles_per_mod, tiles_per_seq):
    m, d = x.shape
    w = w_layer[0]
    tn = TN
    nt = w.shape[2] // 3 // tn
    mod = pl.BlockSpec((1, 1, d), lambda i, j: (i // tiles_per_mod, 0, 0))
    ext = tm + 2 * HALO
    return pl.pallas_call(
        functools.partial(_sconv_in_kernel, tm=tm, tiles_per_seq=tiles_per_seq),
        grid=(m // tm, nt),
        in_specs=_halo_specs(tm, d, m) + [
            mod, mod, _wspec(w_layer, d, tn, 0), _wspec(w_layer, d, tn, nt), _wspec(w_layer, d, tn, 2 * nt),
            pl.BlockSpec((3, tn), lambda i, j: (0, j)),
        ],
        out_specs=pl.BlockSpec((tm, tn), lambda i, j: (i, j)),
        out_shape=jax.ShapeDtypeStruct((m, nt * tn), BF16),
        scratch_shapes=[pltpu.VMEM((ext, d), BF16), pltpu.VMEM((ext, tn), F32)],
        compiler_params=_params("parallel", "arbitrary"),
        name="sconv_in",
    )(x, x, x, shift, scale, w, w, w, conv_w)


def _ffn_up_kernel(x_ref, xp_ref, xn_ref, sh_ref, sc_ref, wg_ref, wv_ref, cg_ref, cv_ref, o_ref,
                   h_scr, yg_scr, yv_scr, *, tm, tiles_per_seq):
    _conv_prologue(x_ref, xp_ref, xn_ref, sh_ref, sc_ref, h_scr, tm=tm, tiles_per_seq=tiles_per_seq)
    rows, out_rows = _strips(tm)
    for lo, hi in rows:
        h = h_scr[lo:hi, :]
        yg_scr[lo:hi, :] = jnp.dot(h, wg_ref[...], preferred_element_type=F32)
        yv_scr[lo:hi, :] = jnp.dot(h, wv_ref[...], preferred_element_type=F32)
    for lo, hi in out_rows:
        ug = _conv3(yg_scr, cg_ref, lo, hi)
        uv = _conv3(yv_scr, cv_ref, lo, hi)
        o_ref[lo:hi, :] = (ug * jax.nn.sigmoid(ug) * uv).astype(o_ref.dtype)


def _ffn_up_call(x, shift, scale, w_layer, conv_w, *, tm, tiles_per_mod, tiles_per_seq):
    m, d = x.shape
    w = w_layer[0]
    tn = TN
    nt = w.shape[2] // 2 // tn
    mod = pl.BlockSpec((1, 1, d), lambda i, j: (i // tiles_per_mod, 0, 0))
    cspec = lambda off: pl.BlockSpec((3, tn), lambda i, j: (0, j + off))
    ext = tm + 2 * HALO
    return pl.pallas_call(
        functools.partial(_ffn_up_kernel, tm=tm, tiles_per_seq=tiles_per_seq),
        grid=(m // tm, nt),
        in_specs=_halo_specs(tm, d, m) + [
            mod, mod, _wspec(w_layer, d, tn, 0), _wspec(w_layer, d, tn, nt), cspec(0), cspec(nt)],
        out_specs=pl.BlockSpec((tm, tn), lambda i, j: (i, j)),
        out_shape=jax.ShapeDtypeStruct((m, nt * tn), BF16),
        scratch_shapes=[pltpu.VMEM((ext, d), BF16), pltpu.VMEM((ext, tn), F32), pltpu.VMEM((ext, tn), F32)],
        compiler_params=_params("parallel", "arbitrary"),
        name="ffn_up",
    )(x, x, x, shift, scale, w, w, conv_w, conv_w)


def _proj_res_kernel(a_ref, w_ref, res_ref, g_ref, o_ref):
    y = jnp.dot(a_ref[...], w_ref[...], preferred_element_type=F32)
    o_ref[...] = res_ref[...] + g_ref[0] * y


def _proj_res_call(a, w_layer, res, gate, *, tm, tiles_per_mod):
    m, k = a.shape
    w = w_layer[0]
    n = w.shape[2]
    tn = n if k * n * 2 <= 8 * 1024 * 1024 else TN
    return pl.pallas_call(
        _proj_res_kernel,
        grid=(m // tm, n // tn),
        in_specs=[
            pl.BlockSpec((tm, k), lambda i, j: (i, 0)),
            _wspec(w_layer, k, tn, 0),
            pl.BlockSpec((tm, tn), lambda i, j: (i, j)),
            pl.BlockSpec((1, 1, tn), lambda i, j: (i // tiles_per_mod, 0, j)),
        ],
        out_specs=pl.BlockSpec((tm, tn), lambda i, j: (i, j)),
        out_shape=jax.ShapeDtypeStruct((m, n), F32),
        compiler_params=_params("parallel", "arbitrary"),
        name="proj_res",
    )(a, w, res, gate)


def _final_norm_kernel(x_ref, g_ref, o_ref):
    x = x_ref[...]
    ms = jnp.mean(x * x, axis=-1, keepdims=True)
    o_ref[...] = (x * lax.rsqrt(ms + EPS)) * g_ref[...]


def _final_norm_call(x, gain, *, tm):
    m, d = x.shape
    return pl.pallas_call(
        _final_norm_kernel,
        grid=(m // tm,),
        in_specs=[pl.BlockSpec((tm, d), lambda i: (i, 0)), pl.BlockSpec((1, d), lambda i: (0, 0))],
        out_specs=pl.BlockSpec((tm, d), lambda i: (i, 0)),
        out_shape=jax.ShapeDtypeStruct((m, d), F32),
        compiler_params=_params("parallel"),
        name="final_norm",
    )(x, gain.reshape(1, d))


def _rope_tables(n_tokens):
    rows = n_tokens // GRID_W
    row = jnp.repeat(jnp.arange(rows, dtype=F32), GRID_W)
    col = jnp.tile(jnp.arange(GRID_W, dtype=F32), rows)
    half = HEAD_DIM // 2
    inv_freq = 1.0 / (ROPE_THETA ** (jnp.arange(0, half, 2, dtype=F32) / half))
    ang_r = row[:, None] * inv_freq
    ang_c = col[:, None] * inv_freq
    zeros = jnp.zeros_like(ang_r)
    cos = jnp.concatenate([jnp.cos(ang_r)] * 2 + [jnp.cos(ang_c)] * 2, axis=-1)
    sin_lo = jnp.concatenate([-jnp.sin(ang_r), zeros, -jnp.sin(ang_c), zeros], axis=-1)
    sin_hi = jnp.concatenate([zeros, jnp.sin(ang_r), zeros, jnp.sin(ang_c)], axis=-1)
    return cos, sin_lo, sin_hi


def kernel(x, c, ctx, c_ctx, w_ada, b_ada, attn_w_qkv, attn_w_o, attn_lambda_q1, attn_lambda_k1,
           attn_lambda_q2, attn_lambda_k2, attn_subln_gain, sconv_w_in, sconv_conv, sconv_w_out,
           ffn_w_up, ffn_conv, ffn_w_down, final_gain):
    b, t, d = x.shape
    lc = ctx.shape[1]
    depth = w_ada.shape[0]
    assert b + 1 <= ADA_ROWS and t % GRID_W == 0
    tm_l = _pick(t, (1024, 512, 256, 128))
    tm_c = _pick(lc, (256, 128))
    lat = dict(tm=tm_l, tiles_per_mod=t // tm_l)
    cx = dict(tm=tm_c, tiles_per_mod=(b * lc) // tm_c)
    lat_seq = dict(tiles_per_seq=t // tm_l, **lat)
    cx_seq = dict(tiles_per_seq=lc // tm_c, **cx)

    cond = jnp.zeros((ADA_ROWS, d), F32).at[:b].set(c).at[b].set(c_ctx)
    mods = _ada_call(cond, w_ada, b_ada).reshape(depth, ADA_ROWS, N_MOD, d)
    tables = [_to_tiles(tbl, tm_l) for tbl in _rope_tables(t)]
    ffn_up_b, ffn_down_b = ffn_w_up.astype(BF16), ffn_w_down.astype(BF16)
    attn_qkv_b, attn_o_b = attn_w_qkv.astype(BF16), attn_w_o.astype(BF16)
    sconv_in_b, sconv_out_b = sconv_w_in.astype(BF16), sconv_w_out.astype(BF16)
    last_attn = max(i for i in range(depth) if i % N_MIXERS == 0)

    xl = _to_tiles(x.reshape(b * t, d), tm_l)
    xc = _to_tiles(ctx.reshape(b * lc, d), tm_c)
    for i in range(depth):
        ctx_update = i < last_attn
        is_attn = i % N_MIXERS == 0
        ctx_read = i <= last_attn and is_attn
        ml = [mods[i, :b, k].reshape(b, 1, d) for k in range(N_MOD)]
        mc = [mods[i, b:b + 1, k].reshape(1, 1, d) for k in range(N_MOD)]
        w_up = (ffn_up_b, i)
        w_down = (ffn_down_b, i)
        if is_attn:
            a = i // N_MIXERS
            lam_init = 0.8 - 0.6 * math.exp(-0.3 * i)
            w_qkv = (attn_qkv_b, a)
            w_o = (attn_o_b, a)
            lams = [p[a].reshape(1, HEAD_DIM) for p in
                    (attn_lambda_q1, attn_lambda_k1, attn_lambda_q2, attn_lambda_k2)]
            gain = attn_subln_gain[a].reshape(1, V_HEAD_DIM)
            qt_l, k_l, vt_l = _qkv_call(xl, ml[0], ml[1], w_qkv, tables, n_seq=b, with_q=True, rope=True, **lat)
            srcs = [(k_l.reshape(b, t, QK_WIDTH), vt_l)]
            if ctx_read:
                outs_c = _qkv_call(xc, mc[0], mc[1], w_qkv, tables, n_seq=b, with_q=ctx_update, rope=False, **cx)
                k_c, vt_c = outs_c[-2].reshape(b, lc, QK_WIDTH), outs_c[-1]
                srcs.append((k_c, vt_c))
            o_l = _attn_call(qt_l, srcs, lams, gain, lam_init=lam_init)
            xl = _proj_res_call(o_l, w_o, xl, ml[2], **lat)
            if ctx_update:
                o_c = _attn_call(outs_c[0], [(k_c, vt_c)], lams, gain, lam_init=lam_init)
                xc = _proj_res_call(o_c, w_o, xc, mc[2], **cx)
        else:
            jx = i // N_MIXERS
            w_in = (sconv_in_b, jx)
            w_out = (sconv_out_b, jx)
            z_l = _sconv_in_call(xl, ml[0], ml[1], w_in, sconv_conv[jx], **lat_seq)
            xl = _proj_res_call(z_l, w_out, xl, ml[2], **lat)
            if ctx_update:
                z_c = _sconv_in_call(xc, mc[0], mc[1], w_in, sconv_conv[jx], **cx_seq)
                xc = _proj_res_call(z_c, w_out, xc, mc[2], **cx)
        u_l = _ffn_up_call(xl, ml[3], ml[4], w_up, ffn_conv[i], **lat_seq)
        xl = _proj_res_call(u_l, w_down, xl, ml[5], **lat)
        if ctx_update:
            u_c = _ffn_up_call(xc, mc[3], mc[4], w_up, ffn_conv[i], **cx_seq)
            xc = _proj_res_call(u_c, w_down, xc, mc[5], **cx)
    return _from_tiles(_final_norm_call(xl, final_gain, tm=tm_l), tm_l).reshape(b, t, d)
```

```python
import functools
import math

import jax
import jax.numpy as jnp
from jax import lax
from jax.experimental import pallas as pl
from jax.experimental.pallas import tpu as pltpu

F32 = jnp.float32
BF16 = jnp.bfloat16

GRID_W = 64
N_MIXERS = 2
DIFF_HEADS = 8
HEAD_DIM = 128
V_HEAD_DIM = 2 * HEAD_DIM
QK_WIDTH = DIFF_HEADS * 2 * HEAD_DIM
V_WIDTH = DIFF_HEADS * V_HEAD_DIM
ROPE_THETA = 10000.0
N_MOD = 6
EPS = 1e-6

LANES = 128
SUBLANES = 8
HALO = 16
V7X_VMEM_LIMIT = 56 * 1024 * 1024
ADA_ROWS = 8
TN = 512
TQ = 512
Q_SCALE = HEAD_DIM ** -0.5 * math.log2(math.e)


def _pick(n, candidates):
    for c in candidates:
        if n % c == 0:
            return c
    return n


def _params(*sem):
    return pltpu.CompilerParams(dimension_semantics=sem, vmem_limit_bytes=V7X_VMEM_LIMIT)


def _norm_mod(x, shift, scale):
    ms = jnp.mean(x * x, axis=-1, keepdims=True)
    return (x * lax.rsqrt(ms + EPS)) * (1.0 + scale) + shift


def _ada_kernel(c_ref, w_ref, b_ref, o_ref):
    c = c_ref[...]
    s = (c * jax.nn.sigmoid(c)).astype(BF16)
    o_ref[0] = jnp.dot(s, w_ref[0].astype(BF16), preferred_element_type=F32) + b_ref[0]


def _ada_call(cond, w_ada, b_ada):
    depth, d, n = w_ada.shape
    tn = _pick(n, (1024, 512, 256, 128))
    return pl.pallas_call(
        _ada_kernel,
        grid=(depth, n // tn),
        in_specs=[
            pl.BlockSpec((ADA_ROWS, d), lambda l, j: (0, 0)),
            pl.BlockSpec((1, d, tn), lambda l, j: (l, 0, j)),
            pl.BlockSpec((1, 1, tn), lambda l, j: (l, 0, j)),
        ],
        out_specs=pl.BlockSpec((1, ADA_ROWS, tn), lambda l, j: (l, 0, j)),
        out_shape=jax.ShapeDtypeStruct((depth, ADA_ROWS, n), F32),
        compiler_params=_params("arbitrary", "arbitrary"),
        name="ada",
    )(cond, w_ada, b_ada.reshape(depth, 1, n))


def _rope(y, cos, sin_lo, sin_hi):
    return y * cos + pltpu.roll(y, LANES - 32, 1) * sin_lo + pltpu.roll(y, 32, 1) * sin_hi


def _qkv_kernel(x_ref, sh_ref, sc_ref, cos_ref, slo_ref, shi_ref, *rest, tn, with_q, rope):
    if with_q:
        wq_ref, wk_ref, wv_ref, qt_ref, k_ref, vt_ref, h_scr = rest
    else:
        wk_ref, wv_ref, k_ref, vt_ref, h_scr = rest

    @pl.when(pl.program_id(1) == 0)
    def _():
        h_scr[...] = _norm_mod(x_ref[...], sh_ref[0], sc_ref[0]).astype(BF16)

    h = h_scr[...]

    def groups(y):
        for g in range(tn // LANES):
            yg = y[:, g * LANES:(g + 1) * LANES]
            if rope:
                yg = _rope(yg, cos_ref[...], slo_ref[...], shi_ref[...])
            yield g, yg

    if with_q:
        tq = qt_ref.shape[2]
        for g, yg in groups(jnp.dot(h, wq_ref[...], preferred_element_type=F32)):
            ygt = (yg * Q_SCALE).T.astype(BF16)
            for blk in range(qt_ref.shape[0]):
                qt_ref[blk, g * LANES:(g + 1) * LANES, :] = ygt[:, blk * tq:(blk + 1) * tq]
    for g, yg in groups(jnp.dot(h, wk_ref[...], preferred_element_type=F32)):
        k_ref[:, g * LANES:(g + 1) * LANES] = yg.astype(BF16)
    vt_ref[...] = jnp.dot(h, wv_ref[...], preferred_element_type=F32).T.astype(BF16)


def _qkv_call(x, shift, scale, w_layer, tables, *, n_seq, tm, tiles_per_mod, with_q, rope):
    m, d = x.shape
    tn = TN
    assert QK_WIDTH == V_WIDTH
    seq = m // n_seq
    tps = seq // tm
    nt = QK_WIDTH // tn
    cos, slo, shi = tables
    pos_tiles = max(cos.shape[0] // tm, 1)
    tbl = pl.BlockSpec((tm, LANES), lambda i, j: (i % pos_tiles, 0))
    mod = pl.BlockSpec((1, 1, d), lambda i, j: (i // tiles_per_mod, 0, 0))
    wspec = lambda off: _wspec(w_layer, d, tn, off)
    w_specs = [wspec(nt), wspec(2 * nt)]
    out_specs = [
        pl.BlockSpec((tm, tn), lambda i, j: (i, j)),
        pl.BlockSpec((None, None, tn, tm), lambda i, j: (i // tps, i % tps, j, 0)),
    ]
    out_shape = [
        jax.ShapeDtypeStruct((m, QK_WIDTH), BF16),
        jax.ShapeDtypeStruct((n_seq, tps, V_WIDTH, tm), BF16),
    ]
    if with_q:
        w_specs.insert(0, wspec(0))
        tq = min(TQ, tm)
        out_specs.insert(0, pl.BlockSpec((None, tm // tq, tn, tq), lambda i, j: (i // tps, i % tps, j, 0)))
        out_shape.insert(0, jax.ShapeDtypeStruct((n_seq, seq // tq, QK_WIDTH, tq), BF16))
    return pl.pallas_call(
        functools.partial(_qkv_kernel, tn=tn, with_q=with_q, rope=rope),
        grid=(m // tm, nt),
        in_specs=[pl.BlockSpec((tm, d), lambda i, j: (i, 0)), mod, mod, tbl, tbl, tbl] + w_specs,
        out_specs=out_specs,
        out_shape=out_shape,
        scratch_shapes=[pltpu.VMEM((tm, d), BF16)],
        compiler_params=_params("parallel", "arbitrary"),
        name="qkv",
    )(x, shift, scale, cos, slo, shi, *([w_layer[0]] * len(w_specs)))


def _scores(k, qt):
    return (jnp.dot(k[:, :HEAD_DIM], qt[:HEAD_DIM, :], preferred_element_type=F32),
            jnp.dot(k[:, HEAD_DIM:], qt[HEAD_DIM:, :], preferred_element_type=F32))


def _colmax(s):
    return jnp.max(s, axis=0, keepdims=True)


def _softmax_update(s, mc, vt, acc, m, l):
    m_new = jnp.maximum(m, mc)
    alpha = jnp.exp2(m - m_new)
    p = jnp.exp2(s - m_new)
    l_new = alpha * l + jnp.sum(p, axis=0, keepdims=True)
    acc[...] = alpha * acc[...] + jnp.dot(vt, p.astype(BF16), preferred_element_type=F32)
    return m_new, l_new


def _attn_update(s, mc, vt, accs, carry):
    m0, l0, m1, l1 = carry
    m0, l0 = _softmax_update(s[0], mc[0], vt, accs[0], m0, l0)
    m1, l1 = _softmax_update(s[1], mc[1], vt, accs[1], m1, l1)
    return m0, l0, m1, l1


def _attn_lambda(lq1_ref, lk1_ref, lq2_ref, lk2_ref, lam_init):
    return (jnp.exp(jnp.sum(lq1_ref[...] * lk1_ref[...], axis=-1, keepdims=True))
            - jnp.exp(jnp.sum(lq2_ref[...] * lk2_ref[...], axis=-1, keepdims=True)) + lam_init)


def _attn_finish(acc0, acc1, l0, l1, lam, gain_ref, lam_init):
    ot = acc0[...] / l0 - lam * (acc1[...] / l1)
    ms = jnp.mean(ot * ot, axis=0, keepdims=True)
    return ((ot * lax.rsqrt(ms + EPS)).T * gain_ref[...] * (1.0 - lam_init)).astype(BF16)


def _attn_init(tq):
    neg = jnp.full((1, tq), -jnp.inf, F32)
    zero = jnp.zeros((1, tq), F32)
    return neg, zero, neg, zero


def _attn_block_kernel(lq1_ref, lk1_ref, lq2_ref, lk2_ref, gain_ref, qt_ref, k_ref, vt_ref, o_ref,
                       acc0, acc1, *, lam_init):
    qt = qt_ref[...]
    acc0[...] = jnp.zeros_like(acc0)
    acc1[...] = jnp.zeros_like(acc1)
    s = _scores(k_ref[...], qt)
    _, l0, _, l1 = _attn_update(s, (_colmax(s[0]), _colmax(s[1])), vt_ref[0], (acc0, acc1),
                                _attn_init(qt.shape[1]))
    lam = _attn_lambda(lq1_ref, lk1_ref, lq2_ref, lk2_ref, lam_init)
    o_ref[...] = _attn_finish(acc0, acc1, l0, l1, lam, gain_ref, lam_init)


def _attn_head_kernel(lq1_ref, lk1_ref, lq2_ref, lk2_ref, gain_ref, q_ref, k_ref, vt_ref, kc_ref, vtc_ref,
                      o_ref, acc0, acc1, fin0, fin1, sa0, sa1, sb0, sb1, *, lam_init):
    nq, _, tq = q_ref.shape
    n_chunks, _, tk = vt_ref.shape
    accs = (acc0, acc1)
    bufs = ((sa0, sa1), (sb0, sb1))
    lam = _attn_lambda(lq1_ref, lk1_ref, lq2_ref, lk2_ref, lam_init)

    def stage(qt, c, buf):
        s = _scores(k_ref[c * tk:(c + 1) * tk, :], qt)
        buf[0][...] = s[0]
        buf[1][...] = s[1]
        return _colmax(s[0]), _colmax(s[1])

    def consume(buf, mc, c, carry):
        return _attn_update((buf[0][...], buf[1][...]), mc, vt_ref[c], accs, carry)

    def store_finished(blk, l0, l1):
        rows = pl.ds(pl.multiple_of(blk * tq, tq), tq)
        o_ref[rows, :] = _attn_finish(fin0, fin1, l0, l1, lam, gain_ref, lam_init)

    def body(qi, state):
        l0_prev, l1_prev, mc = state[0], state[1], state[2:]
        qt = q_ref[qi]
        qt_next = q_ref[jnp.minimum(qi + 1, nq - 1)]
        acc0[...] = jnp.zeros_like(acc0)
        acc1[...] = jnp.zeros_like(acc1)
        carry = _attn_init(tq)
        s_lead = _scores(kc_ref[...], qt)
        mc_next = stage(qt, 1, bufs[1])
        store_finished(jnp.maximum(qi - 1, 0), l0_prev, l1_prev)
        carry = _attn_update(s_lead, (_colmax(s_lead[0]), _colmax(s_lead[1])), vtc_ref[0], accs, carry)
        for c in range(n_chunks):
            mc_cur, cur = (mc, bufs[c % 2]) if c == 0 else (mc_next, bufs[c % 2])
            if 1 <= c < n_chunks - 1:
                mc_after = stage(qt, c + 1, bufs[(c + 1) % 2])
            elif c == n_chunks - 1:
                mc_after = stage(qt_next, 0, bufs[0])
            carry = consume(cur, mc_cur, c, carry)
            if c >= 1:
                mc_next = mc_after
        fin0[...] = acc0[...]
        fin1[...] = acc1[...]
        return (carry[1], carry[3]) + tuple(mc_next)

    fin0[...] = jnp.zeros_like(fin0)
    fin1[...] = jnp.zeros_like(fin1)
    one = jnp.ones((1, tq), F32)
    state = lax.fori_loop(0, nq, body, (one, one) + stage(q_ref[0], 0, bufs[0]))
    store_finished(nq - 1, state[0], state[1])


def _attn_call(qt, srcs, lams, gain, *, lam_init):
    n_seq, nq, _, tq = qt.shape
    hw = V_HEAD_DIM
    const = lambda shape: pl.BlockSpec(shape, lambda *g: (0,) * len(shape))
    small = [const((1, HEAD_DIM))] * 4 + [const((1, hw))]
    acc = pltpu.VMEM((hw, tq), F32)
    out_shape = jax.ShapeDtypeStruct((n_seq * nq * tq, V_WIDTH), BF16)
    if len(srcs) == 1:
        (k, vt), = srcs
        assert vt.shape[1] == 1
        return pl.pallas_call(
            functools.partial(_attn_block_kernel, lam_init=lam_init),
            grid=(n_seq, DIFF_HEADS, nq),
            in_specs=small + [
                pl.BlockSpec((None, None, hw, tq), lambda b, h, i: (b, i, h, 0)),
                pl.BlockSpec((None, k.shape[1], hw), lambda b, h, i: (b, 0, h)),
                pl.BlockSpec((None, 1, hw, vt.shape[3]), lambda b, h, i: (b, 0, h, 0)),
            ],
            out_specs=pl.BlockSpec((tq, hw), lambda b, h, i: (b * nq + i, h)),
            out_shape=out_shape,
            scratch_shapes=[acc, acc],
            compiler_params=_params("parallel", "parallel", "arbitrary"),
            name="diff_attn_block",
        )(*lams, gain, qt, k, vt)
    (k, vt), (kc, vtc) = srcs
    n_chunks, tk = vt.shape[1], vt.shape[3]
    assert n_chunks % 2 == 0 and vtc.shape[1] == 1
    s_scr = pltpu.VMEM((tk, tq), F32)
    per_head = lambda shape: pl.BlockSpec((None,) + shape, lambda b, h: (b, 0, h) + (0,) * (len(shape) - 2))
    return pl.pallas_call(
        functools.partial(_attn_head_kernel, lam_init=lam_init),
        grid=(n_seq, DIFF_HEADS),
        in_specs=small + [
            per_head((nq, hw, tq)),
            pl.BlockSpec((None, k.shape[1], hw), lambda b, h: (b, 0, h)),
            per_head((n_chunks, hw, tk)),
            pl.BlockSpec((None, kc.shape[1], hw), lambda b, h: (b, 0, h)),
            per_head((1, hw, vtc.shape[3])),
        ],
        out_specs=pl.BlockSpec((nq * tq, hw), lambda b, h: (b, h)),
        out_shape=out_shape,
        scratch_shapes=[acc, acc, acc, acc, s_scr, s_scr, s_scr, s_scr],
        compiler_params=_params("parallel", "arbitrary"),
        name="diff_attn",
    )(*lams, gain, qt, k, vt, kc, vtc)


def _to_tiles(x, tm):
    m, d = x.shape
    return x.reshape(m // tm, SUBLANES, tm // SUBLANES, d).swapaxes(1, 2).reshape(m, d)


def _from_tiles(x, tm):
    m, d = x.shape
    return x.reshape(m // tm, tm // SUBLANES, SUBLANES, d).swapaxes(1, 2).reshape(m, d)


def _fill_h(h_scr, x_ref, xp_ref, xn_ref, sh, sc, first, last, tm):
    h = _norm_mod(x_ref[...], sh, sc)
    prev_tok = jnp.where(first, 0.0, _norm_mod(xp_ref[...], sh, sc)[HALO - 1:HALO, :])
    next_tok = jnp.where(last, 0.0, _norm_mod(xn_ref[...], sh, sc)[0:1, :])
    row = lax.broadcasted_iota(jnp.int32, (SUBLANES, h.shape[1]), 0)
    up = jnp.where(row == 0, prev_tok, pltpu.roll(h[tm - SUBLANES:, :], 1, 0))
    down = jnp.where(row == SUBLANES - 1, next_tok, pltpu.roll(h[:SUBLANES, :], SUBLANES - 1, 0))
    zeros = jnp.zeros_like(up)
    h_scr[0:HALO, :] = jnp.concatenate([zeros, up], axis=0).astype(BF16)
    h_scr[HALO:HALO + tm, :] = h.astype(BF16)
    h_scr[HALO + tm:, :] = jnp.concatenate([down, zeros], axis=0).astype(BF16)


def _conv3(y_scr, w_ref, tm):
    return (y_scr[pl.ds(HALO - SUBLANES, tm), :] * w_ref[0:1, :]
            + y_scr[pl.ds(HALO, tm), :] * w_ref[1:2, :]
            + y_scr[pl.ds(HALO + SUBLANES, tm), :] * w_ref[2:3, :])


def _halo_specs(tm, d, m):
    r = tm // HALO
    last_blk = m // HALO - 1
    return [
        pl.BlockSpec((tm, d), lambda i, j: (i, 0)),
        pl.BlockSpec((HALO, d), lambda i, j: (jnp.maximum(i * r - 1, 0), 0)),
        pl.BlockSpec((HALO, d), lambda i, j: (jnp.minimum((i + 1) * r, last_blk), 0)),
    ]


def _conv_prologue(x_ref, xp_ref, xn_ref, sh_ref, sc_ref, h_scr, *, tm, tiles_per_seq):
    @pl.when(pl.program_id(1) == 0)
    def _():
        t = pl.program_id(0) % tiles_per_seq
        _fill_h(h_scr, x_ref, xp_ref, xn_ref, sh_ref[0], sc_ref[0], t == 0, t == tiles_per_seq - 1, tm)


def _wspec(w_layer, d, tn, off):
    layer = w_layer[1]
    return pl.BlockSpec((None, d, tn), lambda i, j: (layer, 0, j + off))


def _sconv_in_kernel(x_ref, xp_ref, xn_ref, sh_ref, sc_ref, wb_ref, wc_ref, wx_ref, cw_ref, o_ref,
                     h_scr, p_scr, *, tm, tiles_per_seq):
    _conv_prologue(x_ref, xp_ref, xn_ref, sh_ref, sc_ref, h_scr, tm=tm, tiles_per_seq=tiles_per_seq)
    h = h_scr[...]
    p_scr[...] = (jnp.dot(h, wc_ref[...], preferred_element_type=F32)
                  * jnp.dot(h, wx_ref[...], preferred_element_type=F32))
    gb = jnp.dot(h_scr[HALO:HALO + tm, :], wb_ref[...], preferred_element_type=F32)
    o_ref[...] = (gb * _conv3(p_scr, cw_ref, tm)).astype(o_ref.dtype)


def _sconv_in_call(x, shift, scale, w_layer, conv_w, *, tm, tiles_per_mod, tiles_per_seq):
    m, d = x.shape
    w = w_layer[0]
    tn = TN
    nt = w.shape[2] // 3 // tn
    mod = pl.BlockSpec((1, 1, d), lambda i, j: (i // tiles_per_mod, 0, 0))
    ext = tm + 2 * HALO
    return pl.pallas_call(
        functools.partial(_sconv_in_kernel, tm=tm, tiles_per_seq=tiles_per_seq),
        grid=(m // tm, nt),
        in_specs=_halo_specs(tm, d, m) + [
            mod, mod, _wspec(w_layer, d, tn, 0), _wspec(w_layer, d, tn, nt), _wspec(w_layer, d, tn, 2 * nt),
            pl.BlockSpec((3, tn), lambda i, j: (0, j)),
        ],
        out_specs=pl.BlockSpec((tm, tn), lambda i, j: (i, j)),
        out_shape=jax.ShapeDtypeStruct((m, nt * tn), BF16),
        scratch_shapes=[pltpu.VMEM((ext, d), BF16), pltpu.VMEM((ext, tn), F32)],
        compiler_params=_params("parallel", "arbitrary"),
        name="sconv_in",
    )(x, x, x, shift, scale, w, w, w, conv_w)


def _ffn_up_kernel(x_ref, xp_ref, xn_ref, sh_ref, sc_ref, wg_ref, wv_ref, cg_ref, cv_ref, o_ref,
                   h_scr, yg_scr, yv_scr, wg_scr, wv_scr, *, tm, tiles_per_seq):
    _conv_prologue(x_ref, xp_ref, xn_ref, sh_ref, sc_ref, h_scr, tm=tm, tiles_per_seq=tiles_per_seq)
    h = h_scr[...]
    wg_scr[...] = wg_ref[...].astype(BF16)
    yg_scr[...] = jnp.dot(h, wg_scr[...], preferred_element_type=F32)
    wv_scr[...] = wv_ref[...].astype(BF16)
    yv_scr[...] = jnp.dot(h, wv_scr[...], preferred_element_type=F32)
    ug = _conv3(yg_scr, cg_ref, tm)
    uv = _conv3(yv_scr, cv_ref, tm)
    o_ref[...] = (ug * jax.nn.sigmoid(ug) * uv).astype(o_ref.dtype)


def _ffn_up_call(x, shift, scale, w_layer, conv_w, *, tm, tiles_per_mod, tiles_per_seq):
    m, d = x.shape
    w = w_layer[0]
    tn = TN
    nt = w.shape[2] // 2 // tn
    mod = pl.BlockSpec((1, 1, d), lambda i, j: (i // tiles_per_mod, 0, 0))
    cspec = lambda off: pl.BlockSpec((3, tn), lambda i, j: (0, j + off))
    ext = tm + 2 * HALO
    return pl.pallas_call(
        functools.partial(_ffn_up_kernel, tm=tm, tiles_per_seq=tiles_per_seq),
        grid=(m // tm, nt),
        in_specs=_halo_specs(tm, d, m) + [
            mod, mod, _wspec(w_layer, d, tn, 0), _wspec(w_layer, d, tn, nt), cspec(0), cspec(nt)],
        out_specs=pl.BlockSpec((tm, tn), lambda i, j: (i, j)),
        out_shape=jax.ShapeDtypeStruct((m, nt * tn), BF16),
        scratch_shapes=[pltpu.VMEM((ext, d), BF16), pltpu.VMEM((ext, tn), F32), pltpu.VMEM((ext, tn), F32),
                        pltpu.VMEM((d, tn), BF16), pltpu.VMEM((d, tn), BF16)],
        compiler_params=_params("parallel", "arbitrary"),
        name="ffn_up",
    )(x, x, x, shift, scale, w, w, conv_w, conv_w)


def _proj_res_kernel(a_ref, w_ref, res_ref, g_ref, o_ref):
    y = jnp.dot(a_ref[...], w_ref[...], preferred_element_type=F32)
    o_ref[...] = res_ref[...] + g_ref[0] * y


def _proj_res_call(a, w_layer, res, gate, *, tm, tiles_per_mod):
    m, k = a.shape
    w = w_layer[0]
    n = w.shape[2]
    tn = n if k * n * 2 <= 8 * 1024 * 1024 else TN
    return pl.pallas_call(
        _proj_res_kernel,
        grid=(m // tm, n // tn),
        in_specs=[
            pl.BlockSpec((tm, k), lambda i, j: (i, 0)),
            _wspec(w_layer, k, tn, 0),
            pl.BlockSpec((tm, tn), lambda i, j: (i, j)),
            pl.BlockSpec((1, 1, tn), lambda i, j: (i // tiles_per_mod, 0, j)),
        ],
        out_specs=pl.BlockSpec((tm, tn), lambda i, j: (i, j)),
        out_shape=jax.ShapeDtypeStruct((m, n), F32),
        compiler_params=_params("parallel", "arbitrary"),
        name="proj_res",
    )(a, w, res, gate)


def _final_norm_kernel(x_ref, g_ref, o_ref):
    x = x_ref[...]
    ms = jnp.mean(x * x, axis=-1, keepdims=True)
    o_ref[...] = (x * lax.rsqrt(ms + EPS)) * g_ref[...]


def _final_norm_call(x, gain, *, tm):
    m, d = x.shape
    return pl.pallas_call(
        _final_norm_kernel,
        grid=(m // tm,),
        in_specs=[pl.BlockSpec((tm, d), lambda i: (i, 0)), pl.BlockSpec((1, d), lambda i: (0, 0))],
        out_specs=pl.BlockSpec((tm, d), lambda i: (i, 0)),
        out_shape=jax.ShapeDtypeStruct((m, d), F32),
        compiler_params=_params("parallel"),
        name="final_norm",
    )(x, gain.reshape(1, d))


def _rope_tables(n_tokens):
    rows = n_tokens // GRID_W
    row = jnp.repeat(jnp.arange(rows, dtype=F32), GRID_W)
    col = jnp.tile(jnp.arange(GRID_W, dtype=F32), rows)
    half = HEAD_DIM // 2
    inv_freq = 1.0 / (ROPE_THETA ** (jnp.arange(0, half, 2, dtype=F32) / half))
    ang_r = row[:, None] * inv_freq
    ang_c = col[:, None] * inv_freq
    zeros = jnp.zeros_like(ang_r)
    cos = jnp.concatenate([jnp.cos(ang_r)] * 2 + [jnp.cos(ang_c)] * 2, axis=-1)
    sin_lo = jnp.concatenate([-jnp.sin(ang_r), zeros, -jnp.sin(ang_c), zeros], axis=-1)
    sin_hi = jnp.concatenate([zeros, jnp.sin(ang_r), zeros, jnp.sin(ang_c)], axis=-1)
    return cos, sin_lo, sin_hi


def kernel(x, c, ctx, c_ctx, w_ada, b_ada, attn_w_qkv, attn_w_o, attn_lambda_q1, attn_lambda_k1,
           attn_lambda_q2, attn_lambda_k2, attn_subln_gain, sconv_w_in, sconv_conv, sconv_w_out,
           ffn_w_up, ffn_conv, ffn_w_down, final_gain):
    b, t, d = x.shape
    lc = ctx.shape[1]
    depth = w_ada.shape[0]
    assert b + 1 <= ADA_ROWS and t % GRID_W == 0
    tm_l = _pick(t, (1024, 512, 256, 128))
    tm_c = _pick(lc, (256, 128))
    lat = dict(tm=tm_l, tiles_per_mod=t // tm_l)
    cx = dict(tm=tm_c, tiles_per_mod=(b * lc) // tm_c)
    lat_seq = dict(tiles_per_seq=t // tm_l, **lat)
    cx_seq = dict(tiles_per_seq=lc // tm_c, **cx)

    cond = jnp.zeros((ADA_ROWS, d), F32).at[:b].set(c).at[b].set(c_ctx)
    mods = _ada_call(cond, w_ada, b_ada).reshape(depth, ADA_ROWS, N_MOD, d)
    tables = [_to_tiles(tbl, tm_l) for tbl in _rope_tables(t)]
    ffn_down_b = ffn_w_down.astype(BF16)
    attn_qkv_b, attn_o_b = attn_w_qkv.astype(BF16), attn_w_o.astype(BF16)
    sconv_in_b, sconv_out_b = sconv_w_in.astype(BF16), sconv_w_out.astype(BF16)
    last_attn = max(i for i in range(depth) if i % N_MIXERS == 0)

    xl = _to_tiles(x.reshape(b * t, d), tm_l)
    xc = _to_tiles(ctx.reshape(b * lc, d), tm_c)
    for i in range(depth):
        ctx_update = i < last_attn
        is_attn = i % N_MIXERS == 0
        ctx_read = i <= last_attn and is_attn
        ml = [mods[i, :b, k].reshape(b, 1, d) for k in range(N_MOD)]
        mc = [mods[i, b:b + 1, k].reshape(1, 1, d) for k in range(N_MOD)]
        w_up = (ffn_w_up, i)
        w_down = (ffn_down_b, i)
        if is_attn:
            a = i // N_MIXERS
            lam_init = 0.8 - 0.6 * math.exp(-0.3 * i)
            w_qkv = (attn_qkv_b, a)
            w_o = (attn_o_b, a)
            lams = [p[a].reshape(1, HEAD_DIM) for p in
                    (attn_lambda_q1, attn_lambda_k1, attn_lambda_q2, attn_lambda_k2)]
            gain = attn_subln_gain[a].reshape(1, V_HEAD_DIM)
            qt_l, k_l, vt_l = _qkv_call(xl, ml[0], ml[1], w_qkv, tables, n_seq=b, with_q=True, rope=True, **lat)
            srcs = [(k_l.reshape(b, t, QK_WIDTH), vt_l)]
            if ctx_read:
                outs_c = _qkv_call(xc, mc[0], mc[1], w_qkv, tables, n_seq=b, with_q=ctx_update, rope=False, **cx)
                k_c, vt_c = outs_c[-2].reshape(b, lc, QK_WIDTH), outs_c[-1]
                srcs.append((k_c, vt_c))
            o_l = _attn_call(qt_l, srcs, lams, gain, lam_init=lam_init)
            xl = _proj_res_call(o_l, w_o, xl, ml[2], **lat)
            if ctx_update:
                o_c = _attn_call(outs_c[0], [(k_c, vt_c)], lams, gain, lam_init=lam_init)
                xc = _proj_res_call(o_c, w_o, xc, mc[2], **cx)
        else:
            jx = i // N_MIXERS
            w_in = (sconv_in_b, jx)
            w_out = (sconv_out_b, jx)
            z_l = _sconv_in_call(xl, ml[0], ml[1], w_in, sconv_conv[jx], **lat_seq)
            xl = _proj_res_call(z_l, w_out, xl, ml[2], **lat)
            if ctx_update:
                z_c = _sconv_in_call(xc, mc[0], mc[1], w_in, sconv_conv[jx], **cx_seq)
                xc = _proj_res_call(z_c, w_out, xc, mc[2], **cx)
        u_l = _ffn_up_call(xl, ml[3], ml[4], w_up, ffn_conv[i], **lat_seq)
        xl = _proj_res_call(u_l, w_down, xl, ml[5], **lat)
        if ctx_update:
            u_c = _ffn_up_call(xc, mc[3], mc[4], w_up, ffn_conv[i], **cx_seq)
            xc = _proj_res_call(u_c, w_down, xc, mc[5], **cx)
    return _from_tiles(_final_norm_call(xl, final_gain, tm=tm_l), tm_l).reshape(b, t, d)
```

```python
import functools
import math

import jax
import jax.numpy as jnp
from jax import lax
from jax.experimental import pallas as pl
from jax.experimental.pallas import tpu as pltpu

F32 = jnp.float32
BF16 = jnp.bfloat16

GRID_W = 64
N_MIXERS = 2
DIFF_HEADS = 8
HEAD_DIM = 128
V_HEAD_DIM = 2 * HEAD_DIM
QK_WIDTH = DIFF_HEADS * 2 * HEAD_DIM
V_WIDTH = DIFF_HEADS * V_HEAD_DIM
ROPE_THETA = 10000.0
N_MOD = 6
EPS = 1e-6

LANES = 128
SUBLANES = 8
HALO = 16
V7X_VMEM_LIMIT = 56 * 1024 * 1024
ADA_ROWS = 8
TN = 512
RESIDENT_WEIGHT_BYTES = 8 * 1024 * 1024
Q_SCALE = HEAD_DIM ** -0.5 * math.log2(math.e)


def _pick(n, candidates):
    for c in candidates:
        if n % c == 0:
            return c
    return n


def _params(*sem):
    return pltpu.CompilerParams(dimension_semantics=sem, vmem_limit_bytes=V7X_VMEM_LIMIT)


def _norm_mod(x, shift, scale):
    ms = jnp.mean(x * x, axis=-1, keepdims=True)
    return (x * lax.rsqrt(ms + EPS)) * (1.0 + scale) + shift


def _ada_kernel(c_ref, w_ref, b_ref, o_ref):
    c = c_ref[...]
    s = (c * jax.nn.sigmoid(c)).astype(BF16)
    o_ref[0] = jnp.dot(s, w_ref[0].astype(BF16), preferred_element_type=F32) + b_ref[0]


def _ada_call(cond, w_ada, b_ada):
    depth, d, n = w_ada.shape
    tn = _pick(n, (1024, 512, 256, 128))
    return pl.pallas_call(
        _ada_kernel,
        grid=(depth, n // tn),
        in_specs=[
            pl.BlockSpec((ADA_ROWS, d), lambda l, j: (0, 0)),
            pl.BlockSpec((1, d, tn), lambda l, j: (l, 0, j)),
            pl.BlockSpec((1, 1, tn), lambda l, j: (l, 0, j)),
        ],
        out_specs=pl.BlockSpec((1, ADA_ROWS, tn), lambda l, j: (l, 0, j)),
        out_shape=jax.ShapeDtypeStruct((depth, ADA_ROWS, n), F32),
        compiler_params=_params("arbitrary", "arbitrary"),
        name="ada",
    )(cond, w_ada, b_ada.reshape(depth, 1, n))


def _rope(y, cos, sin_lo, sin_hi):
    return y * cos + pltpu.roll(y, LANES - 32, 1) * sin_lo + pltpu.roll(y, 32, 1) * sin_hi


def _qkv_kernel(x_ref, sh_ref, sc_ref, cos_ref, slo_ref, shi_ref, *rest, tn, with_q, rope):
    if with_q:
        wq_ref, wk_ref, wv_ref, qt_ref, k_ref, vt_ref, h_scr = rest
    else:
        wk_ref, wv_ref, k_ref, vt_ref, h_scr = rest

    @pl.when(pl.program_id(1) == 0)
    def _():
        h_scr[...] = _norm_mod(x_ref[...], sh_ref[0], sc_ref[0]).astype(BF16)

    h = h_scr[...]

    def groups(y):
        for g in range(tn // LANES):
            yg = y[:, g * LANES:(g + 1) * LANES]
            if rope:
                yg = _rope(yg, cos_ref[...], slo_ref[...], shi_ref[...])
            yield g, yg

    if with_q:
        for g, yg in groups(jnp.dot(h, wq_ref[...], preferred_element_type=F32)):
            qt_ref[g * LANES:(g + 1) * LANES, :] = (yg * Q_SCALE).T.astype(BF16)
    for g, yg in groups(jnp.dot(h, wk_ref[...], preferred_element_type=F32)):
        k_ref[:, g * LANES:(g + 1) * LANES] = yg.astype(BF16)
    vt_ref[...] = jnp.dot(h, wv_ref[...], preferred_element_type=F32).T.astype(BF16)


def _qkv_call(x, shift, scale, w_layer, tables, *, n_seq, tm, tiles_per_mod, with_q, rope):
    m, d = x.shape
    tn = TN
    assert QK_WIDTH == V_WIDTH
    seq = m // n_seq
    tps = seq // tm
    nt = QK_WIDTH // tn
    cos, slo, shi = tables
    pos_tiles = max(cos.shape[0] // tm, 1)
    tbl = pl.BlockSpec((tm, LANES), lambda i, j: (i % pos_tiles, 0))
    mod = pl.BlockSpec((1, 1, d), lambda i, j: (i // tiles_per_mod, 0, 0))
    wspec = lambda off: _wspec(w_layer, d, tn, off)
    w_specs = [wspec(nt), wspec(2 * nt)]
    out_specs = [
        pl.BlockSpec((tm, tn), lambda i, j: (i, j)),
        pl.BlockSpec((None, None, tn, tm), lambda i, j: (i // tps, i % tps, j, 0)),
    ]
    out_shape = [
        jax.ShapeDtypeStruct((m, QK_WIDTH), BF16),
        jax.ShapeDtypeStruct((n_seq, tps, V_WIDTH, tm), BF16),
    ]
    if with_q:
        w_specs.insert(0, wspec(0))
        out_specs.insert(0, pl.BlockSpec((None, tn, tm), lambda i, j: (i // tps, j, i % tps)))
        out_shape.insert(0, jax.ShapeDtypeStruct((n_seq, QK_WIDTH, seq), BF16))
    return pl.pallas_call(
        functools.partial(_qkv_kernel, tn=tn, with_q=with_q, rope=rope),
        grid=(m // tm, nt),
        in_specs=[pl.BlockSpec((tm, d), lambda i, j: (i, 0)), mod, mod, tbl, tbl, tbl] + w_specs,
        out_specs=out_specs,
        out_shape=out_shape,
        scratch_shapes=[pltpu.VMEM((tm, d), BF16)],
        compiler_params=_params("parallel", "arbitrary"),
        name="qkv",
    )(x, shift, scale, cos, slo, shi, *([w_layer[0]] * len(w_specs)))


def _attn_kernel(lq1_ref, lk1_ref, lq2_ref, lk2_ref, gain_ref, qt_ref, *rest, chunks, lam_init):
    n_src = len(chunks)
    srcs = [(rest[2 * s], rest[2 * s + 1]) for s in range(n_src)]
    o_ref, acc0, acc1, sa0, sa1, sb0, sb1 = rest[2 * n_src:]
    qt = qt_ref[...]
    q0t = qt[:HEAD_DIM, :]
    q1t = qt[HEAD_DIM:, :]
    tq = qt.shape[1]
    acc0[...] = jnp.zeros_like(acc0)
    acc1[...] = jnp.zeros_like(acc1)

    def scores(k):
        return (jnp.dot(k[:, :HEAD_DIM], q0t, preferred_element_type=F32),
                jnp.dot(k[:, HEAD_DIM:], q1t, preferred_element_type=F32))

    def update(s, mc, vt, acc, m, l):
        m_new = jnp.maximum(m, mc)
        alpha = jnp.exp2(m - m_new)
        p = jnp.exp2(s - m_new)
        l_new = alpha * l + jnp.sum(p, axis=0, keepdims=True)
        acc[...] = alpha * acc[...] + jnp.dot(vt, p.astype(BF16), preferred_element_type=F32)
        return m_new, l_new

    def colmax(s):
        return jnp.max(s, axis=0, keepdims=True)

    def step(k, vt, carry):
        m0, l0, m1, l1 = carry
        s0, s1 = scores(k)
        m0, l0 = update(s0, colmax(s0), vt, acc0, m0, l0)
        m1, l1 = update(s1, colmax(s1), vt, acc1, m1, l1)
        return m0, l0, m1, l1

    def stage(k, s0_scr, s1_scr):
        s0, s1 = scores(k)
        s0_scr[...] = s0
        s1_scr[...] = s1
        return colmax(s0), colmax(s1)

    def consume(s0_scr, s1_scr, mc, vt, carry):
        m0, l0, m1, l1 = carry
        m0, l0 = update(s0_scr[...], mc[0], vt, acc0, m0, l0)
        m1, l1 = update(s1_scr[...], mc[1], vt, acc1, m1, l1)
        return m0, l0, m1, l1

    neg = jnp.full((1, tq), -jnp.inf, F32)
    zero = jnp.zeros((1, tq), F32)
    carry = (neg, zero, neg, zero)
    for (k_ref, vt_ref), (n_chunks, tk) in zip(srcs, chunks):
        chunk = lambda c, k_ref=k_ref, tk=tk: k_ref[pl.ds(pl.multiple_of(c * tk, tk), tk), :]
        if n_chunks == 1:
            carry = step(k_ref[...], vt_ref[0], carry)
        elif n_chunks % 2 == 0 and sa0.shape[0] == tk:
            def body(i, state, chunk=chunk, vt_ref=vt_ref):
                carry, mca = state[:4], state[4:]
                c = 2 * i
                mcb = stage(chunk(c + 1), sb0, sb1)
                carry = consume(sa0, sa1, mca, vt_ref[c], carry)
                mca = stage(chunk(c + 2), sa0, sa1)
                carry = consume(sb0, sb1, mcb, vt_ref[c + 1], carry)
                return carry + mca
            mca = stage(chunk(0), sa0, sa1)
            state = lax.fori_loop(0, n_chunks // 2 - 1, body, carry + mca)
            carry, mca = state[:4], state[4:]
            mcb = stage(chunk(n_chunks - 1), sb0, sb1)
            carry = consume(sa0, sa1, mca, vt_ref[n_chunks - 2], carry)
            carry = consume(sb0, sb1, mcb, vt_ref[n_chunks - 1], carry)
        else:
            carry = lax.fori_loop(
                0, n_chunks, lambda c, carry, chunk=chunk, vt_ref=vt_ref: step(chunk(c), vt_ref[c], carry), carry)
    _, l0, _, l1 = carry

    lam = (jnp.exp(jnp.sum(lq1_ref[...] * lk1_ref[...], axis=-1, keepdims=True))
           - jnp.exp(jnp.sum(lq2_ref[...] * lk2_ref[...], axis=-1, keepdims=True)) + lam_init)
    ot = acc0[...] / l0 - lam * (acc1[...] / l1)
    ms = jnp.mean(ot * ot, axis=0, keepdims=True)
    o = (ot * lax.rsqrt(ms + EPS)).T * gain_ref[...] * (1.0 - lam_init)
    o_ref[...] = o.astype(o_ref.dtype)


def _attn_call(qt, srcs, lams, gain, *, lam_init):
    n_seq, _, n_q = qt.shape
    tq = _pick(n_q, (512, 256, 128))
    nq = n_q // tq
    hw = V_HEAD_DIM
    lam_spec = pl.BlockSpec((1, HEAD_DIM), lambda b, h, i: (0, 0))
    in_specs = [
        lam_spec, lam_spec, lam_spec, lam_spec,
        pl.BlockSpec((1, hw), lambda b, h, i: (0, 0)),
        pl.BlockSpec((None, hw, tq), lambda b, h, i: (b, h, i)),
    ]
    args, chunks = [], []
    for k, vt in srcs:
        n_chunks, tk = vt.shape[1], vt.shape[3]
        chunks.append((n_chunks, tk))
        in_specs.append(pl.BlockSpec((None, k.shape[1], hw), lambda b, h, i: (b, 0, h)))
        in_specs.append(pl.BlockSpec((None, n_chunks, hw, tk), lambda b, h, i: (b, 0, h, 0)))
        args += [k, vt]
    stage_tk = max(chunks)[1] if max(chunks)[0] > 1 else 8
    s_scr = pltpu.VMEM((stage_tk, tq), F32)
    return pl.pallas_call(
        functools.partial(_attn_kernel, chunks=tuple(chunks), lam_init=lam_init),
        grid=(n_seq, DIFF_HEADS, nq),
        in_specs=in_specs,
        out_specs=pl.BlockSpec((tq, hw), lambda b, h, i: (b * nq + i, h)),
        out_shape=jax.ShapeDtypeStruct((n_seq * n_q, V_WIDTH), BF16),
        scratch_shapes=[pltpu.VMEM((hw, tq), F32), pltpu.VMEM((hw, tq), F32), s_scr, s_scr, s_scr, s_scr],
        compiler_params=_params("parallel", "parallel", "arbitrary"),
        name="diff_attn",
    )(*lams, gain, qt, *args)


def _to_tiles(x, tm):
    m, d = x.shape
    return x.reshape(m // tm, SUBLANES, tm // SUBLANES, d).swapaxes(1, 2).reshape(m, d)


def _from_tiles(x, tm):
    m, d = x.shape
    return x.reshape(m // tm, tm // SUBLANES, SUBLANES, d).swapaxes(1, 2).reshape(m, d)


def _fill_h(h_scr, x_ref, xp_ref, xn_ref, sh, sc, first, last, tm):
    h = _norm_mod(x_ref[...], sh, sc)
    prev_tok = jnp.where(first, 0.0, _norm_mod(xp_ref[...], sh, sc)[HALO - 1:HALO, :])
    next_tok = jnp.where(last, 0.0, _norm_mod(xn_ref[...], sh, sc)[0:1, :])
    row = lax.broadcasted_iota(jnp.int32, (SUBLANES, h.shape[1]), 0)
    up = jnp.where(row == 0, prev_tok, pltpu.roll(h[tm - SUBLANES:, :], 1, 0))
    down = jnp.where(row == SUBLANES - 1, next_tok, pltpu.roll(h[:SUBLANES, :], SUBLANES - 1, 0))
    zeros = jnp.zeros_like(up)
    h_scr[0:HALO, :] = jnp.concatenate([zeros, up], axis=0).astype(BF16)
    h_scr[HALO:HALO + tm, :] = h.astype(BF16)
    h_scr[HALO + tm:, :] = jnp.concatenate([down, zeros], axis=0).astype(BF16)


def _conv3(y_scr, w_ref, tm):
    return (y_scr[pl.ds(HALO - SUBLANES, tm), :] * w_ref[0:1, :]
            + y_scr[pl.ds(HALO, tm), :] * w_ref[1:2, :]
            + y_scr[pl.ds(HALO + SUBLANES, tm), :] * w_ref[2:3, :])


def _halo_specs(tm, d, m, row0=0):
    r = tm // HALO
    last_blk = m // HALO - 1
    return [
        pl.BlockSpec((tm, d), lambda i, j: (i + row0, 0)),
        pl.BlockSpec((HALO, d), lambda i, j: (jnp.maximum((i + row0) * r - 1, 0), 0)),
        pl.BlockSpec((HALO, d), lambda i, j: (jnp.minimum((i + row0 + 1) * r, last_blk), 0)),
    ]


def _conv_prologue(x_ref, xp_ref, xn_ref, sh_ref, sc_ref, h_scr, *, tm, tiles_per_seq, row0=0):
    @pl.when(pl.program_id(1) == 0)
    def _():
        t = (pl.program_id(0) + row0) % tiles_per_seq
        _fill_h(h_scr, x_ref, xp_ref, xn_ref, sh_ref[0], sc_ref[0], t == 0, t == tiles_per_seq - 1, tm)


def _split_rows(call, cast_call):
    out, *w_bf16 = cast_call()
    return call(out, w_bf16), w_bf16


def _wspec(w_layer, d, tn, off):
    layer = w_layer[1]
    return pl.BlockSpec((None, d, tn), lambda i, j: (layer, 0, j + off))


def _sconv_in_kernel(x_ref, xp_ref, xn_ref, sh_ref, sc_ref, wb_ref, wc_ref, wx_ref, cw_ref, o_ref,
                     h_scr, p_scr, *, tm, tiles_per_seq):
    _conv_prologue(x_ref, xp_ref, xn_ref, sh_ref, sc_ref, h_scr, tm=tm, tiles_per_seq=tiles_per_seq)
    h = h_scr[...]
    p_scr[...] = (jnp.dot(h, wc_ref[...], preferred_element_type=F32)
                  * jnp.dot(h, wx_ref[...], preferred_element_type=F32))
    gb = jnp.dot(h_scr[HALO:HALO + tm, :], wb_ref[...], preferred_element_type=F32)
    o_ref[...] = (gb * _conv3(p_scr, cw_ref, tm)).astype(o_ref.dtype)


def _sconv_in_call(x, shift, scale, w_layer, conv_w, *, tm, tiles_per_mod, tiles_per_seq):
    m, d = x.shape
    w = w_layer[0]
    tn = TN
    nt = w.shape[2] // 3 // tn
    mod = pl.BlockSpec((1, 1, d), lambda i, j: (i // tiles_per_mod, 0, 0))
    ext = tm + 2 * HALO
    return pl.pallas_call(
        functools.partial(_sconv_in_kernel, tm=tm, tiles_per_seq=tiles_per_seq),
        grid=(m // tm, nt),
        in_specs=_halo_specs(tm, d, m) + [
            mod, mod, _wspec(w_layer, d, tn, 0), _wspec(w_layer, d, tn, nt), _wspec(w_layer, d, tn, 2 * nt),
            pl.BlockSpec((3, tn), lambda i, j: (0, j)),
        ],
        out_specs=pl.BlockSpec((tm, tn), lambda i, j: (i, j)),
        out_shape=jax.ShapeDtypeStruct((m, nt * tn), BF16),
        scratch_shapes=[pltpu.VMEM((ext, d), BF16), pltpu.VMEM((ext, tn), F32)],
        compiler_params=_params("parallel", "arbitrary"),
        name="sconv_in",
    )(x, x, x, shift, scale, w, w, w, conv_w)


def _ffn_up_kernel(x_ref, xp_ref, xn_ref, sh_ref, sc_ref, wg_ref, wv_ref, cg_ref, cv_ref, *rest,
                   tm, tiles_per_seq, row0, cast, aliased):
    rest = rest[1:] if aliased else rest
    if cast:
        o_ref, wgb_ref, wvb_ref, h_scr, yg_scr, yv_scr = rest
        wgb_ref[...] = wg_ref[...].astype(BF16)
        wvb_ref[...] = wv_ref[...].astype(BF16)
        wg_ref, wv_ref = wgb_ref, wvb_ref
    else:
        o_ref, h_scr, yg_scr, yv_scr = rest
    _conv_prologue(x_ref, xp_ref, xn_ref, sh_ref, sc_ref, h_scr, tm=tm, tiles_per_seq=tiles_per_seq, row0=row0)
    h = h_scr[...]
    yg_scr[...] = jnp.dot(h, wg_ref[...], preferred_element_type=F32)
    yv_scr[...] = jnp.dot(h, wv_ref[...], preferred_element_type=F32)
    ug = _conv3(yg_scr, cg_ref, tm)
    uv = _conv3(yv_scr, cv_ref, tm)
    o_ref[...] = (ug * jax.nn.sigmoid(ug) * uv).astype(o_ref.dtype)


def _ffn_up_call(x, shift, scale, weights, conv_w, *, tm, tiles_per_mod, tiles_per_seq,
                 row0=0, n_rows=None, cast=False, prev_out=None):
    m, d = x.shape
    tn = TN
    nt = conv_w.shape[1] // 2 // tn
    n_rows = m // tm if n_rows is None else n_rows
    mod = pl.BlockSpec((1, 1, d), lambda i, j: ((i + row0) // tiles_per_mod, 0, 0))
    cspec = lambda off: pl.BlockSpec((3, tn), lambda i, j: (0, j + off))
    wspecs = [_wspec((w, layer), d, tn, off) for w, layer, off in weights]
    ext = tm + 2 * HALO
    out_specs = [pl.BlockSpec((tm, tn), lambda i, j: (i + row0, j))]
    out_shape = [jax.ShapeDtypeStruct((m, nt * tn), BF16)]
    if cast:
        out_specs += [pl.BlockSpec((d, tn), lambda i, j: (0, j))] * 2
        out_shape += [jax.ShapeDtypeStruct((d, nt * tn), BF16)] * 2
    args = [x, x, x, shift, scale, weights[0][0], weights[1][0], conv_w, conv_w]
    in_specs = _halo_specs(tm, d, m, row0) + [mod, mod] + wspecs + [cspec(0), cspec(nt)]
    aliases = {}
    if prev_out is not None:
        aliases = {len(args): 0}
        args.append(prev_out)
        in_specs.append(pl.BlockSpec(memory_space=pl.ANY))
    outs = pl.pallas_call(
        functools.partial(_ffn_up_kernel, tm=tm, tiles_per_seq=tiles_per_seq, row0=row0, cast=cast,
                          aliased=prev_out is not None),
        grid=(n_rows, nt),
        in_specs=in_specs,
        out_specs=out_specs,
        out_shape=out_shape,
        input_output_aliases=aliases,
        scratch_shapes=[pltpu.VMEM((ext, d), BF16), pltpu.VMEM((ext, tn), F32), pltpu.VMEM((ext, tn), F32)],
        compiler_params=_params("parallel", "arbitrary"),
        name="ffn_up",
    )(*args)
    return outs if cast else outs[0]


def _proj_res_kernel(a_ref, w_ref, res_ref, g_ref, *rest, cast, aliased):
    rest = rest[1:] if aliased else rest
    if cast:
        o_ref, wb_ref = rest
        wb_ref[...] = w_ref[...].astype(BF16)
        w_ref = wb_ref
    else:
        o_ref, = rest
    y = jnp.dot(a_ref[...], w_ref[...], preferred_element_type=F32)
    o_ref[...] = res_ref[...] + g_ref[0] * y


def _proj_res_call(a, w_layer, res, gate, *, tm, tiles_per_mod, row0=0, n_rows=None, cast=False, prev_out=None):
    m, k = a.shape
    w = w_layer[0]
    n = w.shape[2]
    if cast:
        tn = TN // 2
    else:
        tn = n if k * n * w.dtype.itemsize <= RESIDENT_WEIGHT_BYTES else TN
    n_rows = m // tm if n_rows is None else n_rows
    out_specs = [pl.BlockSpec((tm, tn), lambda i, j: (i + row0, j))]
    out_shape = [jax.ShapeDtypeStruct((m, n), F32)]
    if cast:
        out_specs.append(pl.BlockSpec((k, tn), lambda i, j: (0, j)))
        out_shape.append(jax.ShapeDtypeStruct((k, n), BF16))
    args = [a, w, res, gate]
    in_specs = [
        pl.BlockSpec((tm, k), lambda i, j: (i + row0, 0)),
        _wspec(w_layer, k, tn, 0),
        pl.BlockSpec((tm, tn), lambda i, j: (i + row0, j)),
        pl.BlockSpec((1, 1, tn), lambda i, j: ((i + row0) // tiles_per_mod, 0, j)),
    ]
    aliases = {}
    if prev_out is not None:
        aliases = {len(args): 0}
        args.append(prev_out)
        in_specs.append(pl.BlockSpec(memory_space=pl.ANY))
    outs = pl.pallas_call(
        functools.partial(_proj_res_kernel, cast=cast, aliased=prev_out is not None),
        grid=(n_rows, n // tn),
        in_specs=in_specs,
        out_specs=out_specs,
        out_shape=out_shape,
        input_output_aliases=aliases,
        compiler_params=_params("parallel", "arbitrary"),
        name="proj_res",
    )(*args)
    return outs if cast else outs[0]


def _final_norm_kernel(x_ref, g_ref, o_ref):
    x = x_ref[...]
    ms = jnp.mean(x * x, axis=-1, keepdims=True)
    o_ref[...] = (x * lax.rsqrt(ms + EPS)) * g_ref[...]


def _final_norm_call(x, gain, *, tm):
    m, d = x.shape
    return pl.pallas_call(
        _final_norm_kernel,
        grid=(m // tm,),
        in_specs=[pl.BlockSpec((tm, d), lambda i: (i, 0)), pl.BlockSpec((1, d), lambda i: (0, 0))],
        out_specs=pl.BlockSpec((tm, d), lambda i: (i, 0)),
        out_shape=jax.ShapeDtypeStruct((m, d), F32),
        compiler_params=_params("parallel"),
        name="final_norm",
    )(x, gain.reshape(1, d))


def _rope_tables(n_tokens):
    rows = n_tokens // GRID_W
    row = jnp.repeat(jnp.arange(rows, dtype=F32), GRID_W)
    col = jnp.tile(jnp.arange(GRID_W, dtype=F32), rows)
    half = HEAD_DIM // 2
    inv_freq = 1.0 / (ROPE_THETA ** (jnp.arange(0, half, 2, dtype=F32) / half))
    ang_r = row[:, None] * inv_freq
    ang_c = col[:, None] * inv_freq
    zeros = jnp.zeros_like(ang_r)
    cos = jnp.concatenate([jnp.cos(ang_r)] * 2 + [jnp.cos(ang_c)] * 2, axis=-1)
    sin_lo = jnp.concatenate([-jnp.sin(ang_r), zeros, -jnp.sin(ang_c), zeros], axis=-1)
    sin_hi = jnp.concatenate([zeros, jnp.sin(ang_r), zeros, jnp.sin(ang_c)], axis=-1)
    return cos, sin_lo, sin_hi


def kernel(x, c, ctx, c_ctx, w_ada, b_ada, attn_w_qkv, attn_w_o, attn_lambda_q1, attn_lambda_k1,
           attn_lambda_q2, attn_lambda_k2, attn_subln_gain, sconv_w_in, sconv_conv, sconv_w_out,
           ffn_w_up, ffn_conv, ffn_w_down, final_gain):
    b, t, d = x.shape
    lc = ctx.shape[1]
    depth = w_ada.shape[0]
    assert b + 1 <= ADA_ROWS and t % GRID_W == 0
    tm_l = _pick(t, (1024, 512, 256, 128))
    tm_c = _pick(lc, (256, 128))
    lat = dict(tm=tm_l, tiles_per_mod=t // tm_l)
    cx = dict(tm=tm_c, tiles_per_mod=(b * lc) // tm_c)
    lat_seq = dict(tiles_per_seq=t // tm_l, **lat)
    cx_seq = dict(tiles_per_seq=lc // tm_c, **cx)

    cond = jnp.zeros((ADA_ROWS, d), F32).at[:b].set(c).at[b].set(c_ctx)
    mods = _ada_call(cond, w_ada, b_ada).reshape(depth, ADA_ROWS, N_MOD, d)
    tables = [_to_tiles(tbl, tm_l) for tbl in _rope_tables(t)]
    attn_qkv_b, attn_o_b = attn_w_qkv.astype(BF16), attn_w_o.astype(BF16)
    sconv_in_b, sconv_out_b = sconv_w_in.astype(BF16), sconv_w_out.astype(BF16)
    last_attn = max(i for i in range(depth) if i % N_MIXERS == 0)

    xl = _to_tiles(x.reshape(b * t, d), tm_l)
    xc = _to_tiles(ctx.reshape(b * lc, d), tm_c)
    for i in range(depth):
        ctx_update = i < last_attn
        is_attn = i % N_MIXERS == 0
        ctx_read = i <= last_attn and is_attn
        ml = [mods[i, :b, k].reshape(b, 1, d) for k in range(N_MOD)]
        mc = [mods[i, b:b + 1, k].reshape(1, 1, d) for k in range(N_MOD)]
        nt_ff = ffn_w_down.shape[1] // TN
        if is_attn:
            a = i // N_MIXERS
            lam_init = 0.8 - 0.6 * math.exp(-0.3 * i)
            w_qkv = (attn_qkv_b, a)
            w_o = (attn_o_b, a)
            lams = [p[a].reshape(1, HEAD_DIM) for p in
                    (attn_lambda_q1, attn_lambda_k1, attn_lambda_q2, attn_lambda_k2)]
            gain = attn_subln_gain[a].reshape(1, V_HEAD_DIM)
            qt_l, k_l, vt_l = _qkv_call(xl, ml[0], ml[1], w_qkv, tables, n_seq=b, with_q=True, rope=True, **lat)
            srcs = [(k_l.reshape(b, t, QK_WIDTH), vt_l)]
            if ctx_read:
                outs_c = _qkv_call(xc, mc[0], mc[1], w_qkv, tables, n_seq=b, with_q=ctx_update, rope=False, **cx)
                k_c, vt_c = outs_c[-2].reshape(b, lc, QK_WIDTH), outs_c[-1]
                srcs.append((k_c, vt_c))
            o_l = _attn_call(qt_l, srcs, lams, gain, lam_init=lam_init)
            xl = _proj_res_call(o_l, w_o, xl, ml[2], **lat)
            if ctx_update:
                o_c = _attn_call(outs_c[0], [(k_c, vt_c)], lams, gain, lam_init=lam_init)
                xc = _proj_res_call(o_c, w_o, xc, mc[2], **cx)
        else:
            jx = i // N_MIXERS
            w_in = (sconv_in_b, jx)
            w_out = (sconv_out_b, jx)
            z_l = _sconv_in_call(xl, ml[0], ml[1], w_in, sconv_conv[jx], **lat_seq)
            xl = _proj_res_call(z_l, w_out, xl, ml[2], **lat)
            if ctx_update:
                z_c = _sconv_in_call(xc, mc[0], mc[1], w_in, sconv_conv[jx], **cx_seq)
                xc = _proj_res_call(z_c, w_out, xc, mc[2], **cx)
        rest_rows = dict(row0=1, n_rows=(b * t) // tm_l - 1)
        up_args = (xl, ml[3], ml[4])
        u_l, (wg_b, wv_b) = _split_rows(
            lambda out, wb: _ffn_up_call(*up_args, [(wb[0][None], 0, 0), (wb[1][None], 0, 0)], ffn_conv[i],
                                         prev_out=out, **rest_rows, **lat_seq),
            lambda: _ffn_up_call(*up_args, [(ffn_w_up, i, 0), (ffn_w_up, i, nt_ff)], ffn_conv[i],
                                 n_rows=1, cast=True, **lat_seq))
        w_up = [(wg_b[None], 0, 0), (wv_b[None], 0, 0)]
        down_args = (u_l, xl, ml[5])
        xl, (wd_b,) = _split_rows(
            lambda out, wb: _proj_res_call(down_args[0], (wb[0][None], 0), *down_args[1:],
                                           prev_out=out, **rest_rows, **lat),
            lambda: _proj_res_call(down_args[0], (ffn_w_down, i), *down_args[1:], n_rows=1, cast=True, **lat))
        if ctx_update:
            u_c = _ffn_up_call(xc, mc[3], mc[4], w_up, ffn_conv[i], **cx_seq)
            xc = _proj_res_call(u_c, (wd_b[None], 0), xc, mc[5], **cx)
    return _from_tiles(_final_norm_call(xl, final_gain, tm=tm_l), tm_l).reshape(b, t, d)
```

```python
import functools
import math

import jax
import jax.numpy as jnp
from jax import lax
from jax.experimental import pallas as pl
from jax.experimental.pallas import tpu as pltpu

F32 = jnp.float32
BF16 = jnp.bfloat16

GRID_W = 64
N_MIXERS = 2
DIFF_HEADS = 8
HEAD_DIM = 128
V_HEAD_DIM = 2 * HEAD_DIM
QK_WIDTH = DIFF_HEADS * 2 * HEAD_DIM
V_WIDTH = DIFF_HEADS * V_HEAD_DIM
ROPE_THETA = 10000.0
N_MOD = 6
EPS = 1e-6

LANES = 128
SUBLANES = 8
HALO = 16
V7X_VMEM_LIMIT = 56 * 1024 * 1024
ADA_ROWS = 8
TN = 512
Q_SCALE = HEAD_DIM ** -0.5 * math.log2(math.e)


def _pick(n, candidates):
    for c in candidates:
        if n % c == 0:
            return c
    return n


def _params(*sem):
    return pltpu.CompilerParams(dimension_semantics=sem, vmem_limit_bytes=V7X_VMEM_LIMIT)


def _norm_mod(x, shift, scale):
    ms = jnp.mean(x * x, axis=-1, keepdims=True)
    return (x * lax.rsqrt(ms + EPS)) * (1.0 + scale) + shift


def _ada_kernel(c_ref, w_ref, b_ref, o_ref):
    c = c_ref[...]
    s = (c * jax.nn.sigmoid(c)).astype(BF16)
    o_ref[0] = jnp.dot(s, w_ref[0].astype(BF16), preferred_element_type=F32) + b_ref[0]


def _ada_call(cond, w_ada, b_ada):
    depth, d, n = w_ada.shape
    tn = _pick(n, (1024, 512, 256, 128))
    return pl.pallas_call(
        _ada_kernel,
        grid=(depth, n // tn),
        in_specs=[
            pl.BlockSpec((ADA_ROWS, d), lambda l, j: (0, 0)),
            pl.BlockSpec((1, d, tn), lambda l, j: (l, 0, j)),
            pl.BlockSpec((1, 1, tn), lambda l, j: (l, 0, j)),
        ],
        out_specs=pl.BlockSpec((1, ADA_ROWS, tn), lambda l, j: (l, 0, j)),
        out_shape=jax.ShapeDtypeStruct((depth, ADA_ROWS, n), F32),
        compiler_params=_params("arbitrary", "arbitrary"),
        name="ada",
    )(cond, w_ada, b_ada.reshape(depth, 1, n))


def _rope(y, cos, sin_lo, sin_hi):
    return y * cos + pltpu.roll(y, LANES - 32, 1) * sin_lo + pltpu.roll(y, 32, 1) * sin_hi


def _qkv_kernel(x_ref, sh_ref, sc_ref, cos_ref, slo_ref, shi_ref, *rest, tn, with_q, rope):
    if with_q:
        wq_ref, wk_ref, wv_ref, qt_ref, k_ref, vt_ref, h_scr = rest
    else:
        wk_ref, wv_ref, k_ref, vt_ref, h_scr = rest

    @pl.when(pl.program_id(1) == 0)
    def _():
        h_scr[...] = _norm_mod(x_ref[...], sh_ref[0], sc_ref[0]).astype(BF16)

    h = h_scr[...]

    def groups(y):
        for g in range(tn // LANES):
            yg = y[:, g * LANES:(g + 1) * LANES]
            if rope:
                yg = _rope(yg, cos_ref[...], slo_ref[...], shi_ref[...])
            yield g, yg

    if with_q:
        for g, yg in groups(jnp.dot(h, wq_ref[...], preferred_element_type=F32)):
            qt_ref[g * LANES:(g + 1) * LANES, :] = (yg * Q_SCALE).T.astype(BF16)
    for g, yg in groups(jnp.dot(h, wk_ref[...], preferred_element_type=F32)):
        k_ref[:, g * LANES:(g + 1) * LANES] = yg.astype(BF16)
    vt_ref[...] = jnp.dot(h, wv_ref[...], preferred_element_type=F32).T.astype(BF16)


def _qkv_call(x, shift, scale, w_layer, tables, *, n_seq, tm, tiles_per_mod, with_q, rope):
    m, d = x.shape
    tn = TN
    assert QK_WIDTH == V_WIDTH
    seq = m // n_seq
    tps = seq // tm
    nt = QK_WIDTH // tn
    cos, slo, shi = tables
    pos_tiles = max(cos.shape[0] // tm, 1)
    tbl = pl.BlockSpec((tm, LANES), lambda i, j: (i % pos_tiles, 0))
    mod = pl.BlockSpec((1, 1, d), lambda i, j: (i // tiles_per_mod, 0, 0))
    wspec = lambda off: _wspec(w_layer, d, tn, off)
    w_specs = [wspec(nt), wspec(2 * nt)]
    out_specs = [
        pl.BlockSpec((tm, tn), lambda i, j: (i, j)),
        pl.BlockSpec((None, None, tn, tm), lambda i, j: (i // tps, i % tps, j, 0)),
    ]
    out_shape = [
        jax.ShapeDtypeStruct((m, QK_WIDTH), BF16),
        jax.ShapeDtypeStruct((n_seq, tps, V_WIDTH, tm), BF16),
    ]
    if with_q:
        w_specs.insert(0, wspec(0))
        out_specs.insert(0, pl.BlockSpec((None, tn, tm), lambda i, j: (i // tps, j, i % tps)))
        out_shape.insert(0, jax.ShapeDtypeStruct((n_seq, QK_WIDTH, seq), BF16))
    return pl.pallas_call(
        functools.partial(_qkv_kernel, tn=tn, with_q=with_q, rope=rope),
        grid=(m // tm, nt),
        in_specs=[pl.BlockSpec((tm, d), lambda i, j: (i, 0)), mod, mod, tbl, tbl, tbl] + w_specs,
        out_specs=out_specs,
        out_shape=out_shape,
        scratch_shapes=[pltpu.VMEM((tm, d), BF16)],
        compiler_params=_params("parallel", "arbitrary"),
        name="qkv",
    )(x, shift, scale, cos, slo, shi, *([w_layer[0]] * len(w_specs)))


def _attn_kernel(lq1_ref, lk1_ref, lq2_ref, lk2_ref, gain_ref, qt_ref, *rest, chunks, lam_init):
    n_src = len(chunks)
    srcs = [(rest[2 * s], rest[2 * s + 1]) for s in range(n_src)]
    o_ref, acc0, acc1, sa0, sa1, sb0, sb1 = rest[2 * n_src:]
    qt = qt_ref[...]
    q0t = qt[:HEAD_DIM, :]
    q1t = qt[HEAD_DIM:, :]
    tq = qt.shape[1]
    acc0[...] = jnp.zeros_like(acc0)
    acc1[...] = jnp.zeros_like(acc1)

    def scores(k):
        return (jnp.dot(k[:, :HEAD_DIM], q0t, preferred_element_type=F32),
                jnp.dot(k[:, HEAD_DIM:], q1t, preferred_element_type=F32))

    def update(s, mc, vt, acc, m, l):
        m_new = jnp.maximum(m, mc)
        alpha = jnp.exp2(m - m_new)
        p = jnp.exp2(s - m_new)
        l_new = alpha * l + jnp.sum(p, axis=0, keepdims=True)
        acc[...] = alpha * acc[...] + jnp.dot(vt, p.astype(BF16), preferred_element_type=F32)
        return m_new, l_new

    def colmax(s):
        return jnp.max(s, axis=0, keepdims=True)

    def step(k, vt, carry):
        m0, l0, m1, l1 = carry
        s0, s1 = scores(k)
        m0, l0 = update(s0, colmax(s0), vt, acc0, m0, l0)
        m1, l1 = update(s1, colmax(s1), vt, acc1, m1, l1)
        return m0, l0, m1, l1

    def stage(k, s0_scr, s1_scr):
        s0, s1 = scores(k)
        s0_scr[...] = s0
        s1_scr[...] = s1
        return colmax(s0), colmax(s1)

    def consume(s0_scr, s1_scr, mc, vt, carry):
        m0, l0, m1, l1 = carry
        m0, l0 = update(s0_scr[...], mc[0], vt, acc0, m0, l0)
        m1, l1 = update(s1_scr[...], mc[1], vt, acc1, m1, l1)
        return m0, l0, m1, l1

    neg = jnp.full((1, tq), -jnp.inf, F32)
    zero = jnp.zeros((1, tq), F32)
    carry = (neg, zero, neg, zero)
    order = sorted(range(n_src), key=lambda s: chunks[s][0])
    lead = None
    for s in order:
        (k_ref, vt_ref), (n_chunks, tk) = srcs[s], chunks[s]
        chunk = lambda c, k_ref=k_ref, tk=tk: k_ref[pl.ds(pl.multiple_of(c * tk, tk), tk), :]
        if n_chunks == 1 and lead is None and s != order[-1]:
            lead = (k_ref[...], vt_ref[0])
        elif n_chunks == 1:
            carry = step(k_ref[...], vt_ref[0], carry)
        elif n_chunks % 2 == 0 and sa0.shape[0] == tk:
            def body(i, state, chunk=chunk, vt_ref=vt_ref):
                carry, mca = state[:4], state[4:]
                c = 2 * i
                mcb = stage(chunk(c + 1), sb0, sb1)
                carry = consume(sa0, sa1, mca, vt_ref[c], carry)
                mca = stage(chunk(c + 2), sa0, sa1)
                carry = consume(sb0, sb1, mcb, vt_ref[c + 1], carry)
                return carry + mca
            if lead is not None:
                s0, s1 = scores(lead[0])
            mca = stage(chunk(0), sa0, sa1)
            if lead is not None:
                m0, l0, m1, l1 = carry
                m0, l0 = update(s0, colmax(s0), lead[1], acc0, m0, l0)
                m1, l1 = update(s1, colmax(s1), lead[1], acc1, m1, l1)
                carry, lead = (m0, l0, m1, l1), None
            state = lax.fori_loop(0, n_chunks // 2 - 1, body, carry + mca)
            carry, mca = state[:4], state[4:]
            mcb = stage(chunk(n_chunks - 1), sb0, sb1)
            carry = consume(sa0, sa1, mca, vt_ref[n_chunks - 2], carry)
            carry = consume(sb0, sb1, mcb, vt_ref[n_chunks - 1], carry)
        else:
            carry = lax.fori_loop(
                0, n_chunks, lambda c, carry, chunk=chunk, vt_ref=vt_ref: step(chunk(c), vt_ref[c], carry), carry)
    if lead is not None:
        carry = step(*lead, carry)
    _, l0, _, l1 = carry

    lam = (jnp.exp(jnp.sum(lq1_ref[...] * lk1_ref[...], axis=-1, keepdims=True))
           - jnp.exp(jnp.sum(lq2_ref[...] * lk2_ref[...], axis=-1, keepdims=True)) + lam_init)
    ot = acc0[...] / l0 - lam * (acc1[...] / l1)
    ms = jnp.mean(ot * ot, axis=0, keepdims=True)
    o = (ot * lax.rsqrt(ms + EPS)).T * gain_ref[...] * (1.0 - lam_init)
    o_ref[...] = o.astype(o_ref.dtype)


def _attn_call(qt, srcs, lams, gain, *, lam_init):
    n_seq, _, n_q = qt.shape
    tq = _pick(n_q, (512, 256, 128))
    nq = n_q // tq
    hw = V_HEAD_DIM
    lam_spec = pl.BlockSpec((1, HEAD_DIM), lambda b, h, i: (0, 0))
    in_specs = [
        lam_spec, lam_spec, lam_spec, lam_spec,
        pl.BlockSpec((1, hw), lambda b, h, i: (0, 0)),
        pl.BlockSpec((None, hw, tq), lambda b, h, i: (b, h, i)),
    ]
    args, chunks = [], []
    for k, vt in srcs:
        n_chunks, tk = vt.shape[1], vt.shape[3]
        chunks.append((n_chunks, tk))
        in_specs.append(pl.BlockSpec((None, k.shape[1], hw), lambda b, h, i: (b, 0, h)))
        in_specs.append(pl.BlockSpec((None, n_chunks, hw, tk), lambda b, h, i: (b, 0, h, 0)))
        args += [k, vt]
    stage_tk = max(chunks)[1] if max(chunks)[0] > 1 else 8
    s_scr = pltpu.VMEM((stage_tk, tq), F32)
    return pl.pallas_call(
        functools.partial(_attn_kernel, chunks=tuple(chunks), lam_init=lam_init),
        grid=(n_seq, DIFF_HEADS, nq),
        in_specs=in_specs,
        out_specs=pl.BlockSpec((tq, hw), lambda b, h, i: (b * nq + i, h)),
        out_shape=jax.ShapeDtypeStruct((n_seq * n_q, V_WIDTH), BF16),
        scratch_shapes=[pltpu.VMEM((hw, tq), F32), pltpu.VMEM((hw, tq), F32), s_scr, s_scr, s_scr, s_scr],
        compiler_params=_params("parallel", "parallel", "arbitrary"),
        name="diff_attn",
    )(*lams, gain, qt, *args)


def _to_tiles(x, tm):
    m, d = x.shape
    return x.reshape(m // tm, SUBLANES, tm // SUBLANES, d).swapaxes(1, 2).reshape(m, d)


def _from_tiles(x, tm):
    m, d = x.shape
    return x.reshape(m // tm, tm // SUBLANES, SUBLANES, d).swapaxes(1, 2).reshape(m, d)


def _fill_h(h_scr, x_ref, xp_ref, xn_ref, sh, sc, first, last, tm):
    h = _norm_mod(x_ref[...], sh, sc)
    prev_tok = jnp.where(first, 0.0, _norm_mod(xp_ref[...], sh, sc)[HALO - 1:HALO, :])
    next_tok = jnp.where(last, 0.0, _norm_mod(xn_ref[...], sh, sc)[0:1, :])
    row = lax.broadcasted_iota(jnp.int32, (SUBLANES, h.shape[1]), 0)
    up = jnp.where(row == 0, prev_tok, pltpu.roll(h[tm - SUBLANES:, :], 1, 0))
    down = jnp.where(row == SUBLANES - 1, next_tok, pltpu.roll(h[:SUBLANES, :], SUBLANES - 1, 0))
    zeros = jnp.zeros_like(up)
    h_scr[0:HALO, :] = jnp.concatenate([zeros, up], axis=0).astype(BF16)
    h_scr[HALO:HALO + tm, :] = h.astype(BF16)
    h_scr[HALO + tm:, :] = jnp.concatenate([down, zeros], axis=0).astype(BF16)


def _conv3(y_scr, w_ref, tm):
    return (y_scr[pl.ds(HALO - SUBLANES, tm), :] * w_ref[0:1, :]
            + y_scr[pl.ds(HALO, tm), :] * w_ref[1:2, :]
            + y_scr[pl.ds(HALO + SUBLANES, tm), :] * w_ref[2:3, :])


def _halo_specs(tm, d, m):
    r = tm // HALO
    last_blk = m // HALO - 1
    return [
        pl.BlockSpec((tm, d), lambda i, j: (i, 0)),
        pl.BlockSpec((HALO, d), lambda i, j: (jnp.maximum(i * r - 1, 0), 0)),
        pl.BlockSpec((HALO, d), lambda i, j: (jnp.minimum((i + 1) * r, last_blk), 0)),
    ]


def _conv_prologue(x_ref, xp_ref, xn_ref, sh_ref, sc_ref, h_scr, *, tm, tiles_per_seq):
    @pl.when(pl.program_id(1) == 0)
    def _():
        t = pl.program_id(0) % tiles_per_seq
        _fill_h(h_scr, x_ref, xp_ref, xn_ref, sh_ref[0], sc_ref[0], t == 0, t == tiles_per_seq - 1, tm)


def _wspec(w_layer, d, tn, off):
    layer = w_layer[1]
    return pl.BlockSpec((None, d, tn), lambda i, j: (layer, 0, j + off))


def _sconv_in_kernel(x_ref, xp_ref, xn_ref, sh_ref, sc_ref, wb_ref, wc_ref, wx_ref, cw_ref, o_ref,
                     h_scr, p_scr, *, tm, tiles_per_seq):
    _conv_prologue(x_ref, xp_ref, xn_ref, sh_ref, sc_ref, h_scr, tm=tm, tiles_per_seq=tiles_per_seq)
    h = h_scr[...]
    p_scr[...] = (jnp.dot(h, wc_ref[...], preferred_element_type=F32)
                  * jnp.dot(h, wx_ref[...], preferred_element_type=F32))
    gb = jnp.dot(h_scr[HALO:HALO + tm, :], wb_ref[...], preferred_element_type=F32)
    o_ref[...] = (gb * _conv3(p_scr, cw_ref, tm)).astype(o_ref.dtype)


def _sconv_in_call(x, shift, scale, w_layer, conv_w, *, tm, tiles_per_mod, tiles_per_seq):
    m, d = x.shape
    w = w_layer[0]
    tn = TN
    nt = w.shape[2] // 3 // tn
    mod = pl.BlockSpec((1, 1, d), lambda i, j: (i // tiles_per_mod, 0, 0))
    ext = tm + 2 * HALO
    return pl.pallas_call(
        functools.partial(_sconv_in_kernel, tm=tm, tiles_per_seq=tiles_per_seq),
        grid=(m // tm, nt),
        in_specs=_halo_specs(tm, d, m) + [
            mod, mod, _wspec(w_layer, d, tn, 0), _wspec(w_layer, d, tn, nt), _wspec(w_layer, d, tn, 2 * nt),
            pl.BlockSpec((3, tn), lambda i, j: (0, j)),
        ],
        out_specs=pl.BlockSpec((tm, tn), lambda i, j: (i, j)),
        out_shape=jax.ShapeDtypeStruct((m, nt * tn), BF16),
        scratch_shapes=[pltpu.VMEM((ext, d), BF16), pltpu.VMEM((ext, tn), F32)],
        compiler_params=_params("parallel", "arbitrary"),
        name="sconv_in",
    )(x, x, x, shift, scale, w, w, w, conv_w)


def _ffn_up_kernel(x_ref, xp_ref, xn_ref, sh_ref, sc_ref, wg_ref, wv_ref, cg_ref, cv_ref, o_ref,
                   h_scr, yg_scr, yv_scr, *, tm, tiles_per_seq):
    _conv_prologue(x_ref, xp_ref, xn_ref, sh_ref, sc_ref, h_scr, tm=tm, tiles_per_seq=tiles_per_seq)
    h = h_scr[...]
    yg_scr[...] = jnp.dot(h, wg_ref[...], preferred_element_type=F32)
    yv_scr[...] = jnp.dot(h, wv_ref[...], preferred_element_type=F32)
    ug = _conv3(yg_scr, cg_ref, tm)
    uv = _conv3(yv_scr, cv_ref, tm)
    o_ref[...] = (ug * jax.nn.sigmoid(ug) * uv).astype(o_ref.dtype)


def _ffn_up_call(x, shift, scale, w_layer, conv_w, *, tm, tiles_per_mod, tiles_per_seq):
    m, d = x.shape
    w = w_layer[0]
    tn = TN
    nt = w.shape[2] // 2 // tn
    mod = pl.BlockSpec((1, 1, d), lambda i, j: (i // tiles_per_mod, 0, 0))
    cspec = lambda off: pl.BlockSpec((3, tn), lambda i, j: (0, j + off))
    ext = tm + 2 * HALO
    return pl.pallas_call(
        functools.partial(_ffn_up_kernel, tm=tm, tiles_per_seq=tiles_per_seq),
        grid=(m // tm, nt),
        in_specs=_halo_specs(tm, d, m) + [
            mod, mod, _wspec(w_layer, d, tn, 0), _wspec(w_layer, d, tn, nt), cspec(0), cspec(nt)],
        out_specs=pl.BlockSpec((tm, tn), lambda i, j: (i, j)),
        out_shape=jax.ShapeDtypeStruct((m, nt * tn), BF16),
        scratch_shapes=[pltpu.VMEM((ext, d), BF16), pltpu.VMEM((ext, tn), F32), pltpu.VMEM((ext, tn), F32)],
        compiler_params=_params("parallel", "arbitrary"),
        name="ffn_up",
    )(x, x, x, shift, scale, w, w, conv_w, conv_w)


def _proj_res_kernel(a_ref, w_ref, res_ref, g_ref, o_ref):
    y = jnp.dot(a_ref[...], w_ref[...], preferred_element_type=F32)
    o_ref[...] = res_ref[...] + g_ref[0] * y


def _proj_res_stream_kernel(a_hbm, w_ref, res_ref, g_ref, o_ref, a_scr, sem, *, tm):
    i = pl.program_id(0)
    slot = i % 2

    def fetch(row, dst):
        rows = pl.ds(pl.multiple_of(row * tm, tm), tm)
        return pltpu.make_async_copy(a_hbm.at[rows, :], a_scr.at[dst], sem.at[dst])

    @pl.when(pl.program_id(1) == 0)
    def _():
        @pl.when(i == 0)
        def _():
            fetch(0, 0).start()

        fetch(i, slot).wait()

        @pl.when(i + 1 < pl.num_programs(0))
        def _():
            fetch(i + 1, 1 - slot).start()

    y = jnp.dot(a_scr[slot], w_ref[...], preferred_element_type=F32)
    o_ref[...] = res_ref[...] + g_ref[0] * y


def _proj_res_call(a, w_layer, res, gate, *, tm, tiles_per_mod):
    m, k = a.shape
    w = w_layer[0]
    n = w.shape[2]
    tn = n if k * n * 2 <= 8 * 1024 * 1024 else TN
    if n // tn > 1:
        return pl.pallas_call(
            functools.partial(_proj_res_stream_kernel, tm=tm),
            grid=(m // tm, n // tn),
            in_specs=[
                pl.BlockSpec(memory_space=pl.ANY),
                _wspec(w_layer, k, tn, 0),
                pl.BlockSpec((tm, tn), lambda i, j: (i, j)),
                pl.BlockSpec((1, 1, tn), lambda i, j: (i // tiles_per_mod, 0, j)),
            ],
            out_specs=pl.BlockSpec((tm, tn), lambda i, j: (i, j)),
            out_shape=jax.ShapeDtypeStruct((m, n), F32),
            scratch_shapes=[pltpu.VMEM((2, tm, k), BF16), pltpu.SemaphoreType.DMA((2,))],
            compiler_params=_params("arbitrary", "arbitrary"),
            name="proj_res_stream",
        )(a, w, res, gate)
    return pl.pallas_call(
        _proj_res_kernel,
        grid=(m // tm, n // tn),
        in_specs=[
            pl.BlockSpec((tm, k), lambda i, j: (i, 0)),
            _wspec(w_layer, k, tn, 0),
            pl.BlockSpec((tm, tn), lambda i, j: (i, j)),
            pl.BlockSpec((1, 1, tn), lambda i, j: (i // tiles_per_mod, 0, j)),
        ],
        out_specs=pl.BlockSpec((tm, tn), lambda i, j: (i, j)),
        out_shape=jax.ShapeDtypeStruct((m, n), F32),
        compiler_params=_params("parallel", "arbitrary"),
        name="proj_res",
    )(a, w, res, gate)


def _final_norm_kernel(x_ref, g_ref, o_ref):
    x = x_ref[...]
    ms = jnp.mean(x * x, axis=-1, keepdims=True)
    o_ref[...] = (x * lax.rsqrt(ms + EPS)) * g_ref[...]


def _final_norm_call(x, gain, *, tm):
    m, d = x.shape
    return pl.pallas_call(
        _final_norm_kernel,
        grid=(m // tm,),
        in_specs=[pl.BlockSpec((tm, d), lambda i: (i, 0)), pl.BlockSpec((1, d), lambda i: (0, 0))],
        out_specs=pl.BlockSpec((tm, d), lambda i: (i, 0)),
        out_shape=jax.ShapeDtypeStruct((m, d), F32),
        compiler_params=_params("parallel"),
        name="final_norm",
    )(x, gain.reshape(1, d))


def _rope_tables(n_tokens):
    rows = n_tokens // GRID_W
    row = jnp.repeat(jnp.arange(rows, dtype=F32), GRID_W)
    col = jnp.tile(jnp.arange(GRID_W, dtype=F32), rows)
    half = HEAD_DIM // 2
    inv_freq = 1.0 / (ROPE_THETA ** (jnp.arange(0, half, 2, dtype=F32) / half))
    ang_r = row[:, None] * inv_freq
    ang_c = col[:, None] * inv_freq
    zeros = jnp.zeros_like(ang_r)
    cos = jnp.concatenate([jnp.cos(ang_r)] * 2 + [jnp.cos(ang_c)] * 2, axis=-1)
    sin_lo = jnp.concatenate([-jnp.sin(ang_r), zeros, -jnp.sin(ang_c), zeros], axis=-1)
    sin_hi = jnp.concatenate([zeros, jnp.sin(ang_r), zeros, jnp.sin(ang_c)], axis=-1)
    return cos, sin_lo, sin_hi


def kernel(x, c, ctx, c_ctx, w_ada, b_ada, attn_w_qkv, attn_w_o, attn_lambda_q1, attn_lambda_k1,
           attn_lambda_q2, attn_lambda_k2, attn_subln_gain, sconv_w_in, sconv_conv, sconv_w_out,
           ffn_w_up, ffn_conv, ffn_w_down, final_gain):
    b, t, d = x.shape
    lc = ctx.shape[1]
    depth = w_ada.shape[0]
    assert b + 1 <= ADA_ROWS and t % GRID_W == 0
    tm_l = _pick(t, (1024, 512, 256, 128))
    tm_c = _pick(lc, (256, 128))
    lat = dict(tm=tm_l, tiles_per_mod=t // tm_l)
    cx = dict(tm=tm_c, tiles_per_mod=(b * lc) // tm_c)
    lat_seq = dict(tiles_per_seq=t // tm_l, **lat)
    cx_seq = dict(tiles_per_seq=lc // tm_c, **cx)

    cond = jnp.zeros((ADA_ROWS, d), F32).at[:b].set(c).at[b].set(c_ctx)
    mods = _ada_call(cond, w_ada, b_ada).reshape(depth, ADA_ROWS, N_MOD, d)
    tables = [_to_tiles(tbl, tm_l) for tbl in _rope_tables(t)]
    ffn_up_b, ffn_down_b = ffn_w_up.astype(BF16), ffn_w_down.astype(BF16)
    attn_qkv_b, attn_o_b = attn_w_qkv.astype(BF16), attn_w_o.astype(BF16)
    sconv_in_b, sconv_out_b = sconv_w_in.astype(BF16), sconv_w_out.astype(BF16)
    last_attn = max(i for i in range(depth) if i % N_MIXERS == 0)

    xl = _to_tiles(x.reshape(b * t, d), tm_l)
    xc = _to_tiles(ctx.reshape(b * lc, d), tm_c)
    for i in range(depth):
        ctx_update = i < last_attn
        is_attn = i % N_MIXERS == 0
        ctx_read = i <= last_attn and is_attn
        ml = [mods[i, :b, k].reshape(b, 1, d) for k in range(N_MOD)]
        mc = [mods[i, b:b + 1, k].reshape(1, 1, d) for k in range(N_MOD)]
        w_up = (ffn_up_b, i)
        w_down = (ffn_down_b, i)
        if is_attn:
            a = i // N_MIXERS
            lam_init = 0.8 - 0.6 * math.exp(-0.3 * i)
            w_qkv = (attn_qkv_b, a)
            w_o = (attn_o_b, a)
            lams = [p[a].reshape(1, HEAD_DIM) for p in
                    (attn_lambda_q1, attn_lambda_k1, attn_lambda_q2, attn_lambda_k2)]
            gain = attn_subln_gain[a].reshape(1, V_HEAD_DIM)
            qt_l, k_l, vt_l = _qkv_call(xl, ml[0], ml[1], w_qkv, tables, n_seq=b, with_q=True, rope=True, **lat)
            srcs = [(k_l.reshape(b, t, QK_WIDTH), vt_l)]
            if ctx_read:
                outs_c = _qkv_call(xc, mc[0], mc[1], w_qkv, tables, n_seq=b, with_q=ctx_update, rope=False, **cx)
                k_c, vt_c = outs_c[-2].reshape(b, lc, QK_WIDTH), outs_c[-1]
                srcs.append((k_c, vt_c))
            o_l = _attn_call(qt_l, srcs, lams, gain, lam_init=lam_init)
            xl = _proj_res_call(o_l, w_o, xl, ml[2], **lat)
            if ctx_update:
                o_c = _attn_call(outs_c[0], [(k_c, vt_c)], lams, gain, lam_init=lam_init)
                xc = _proj_res_call(o_c, w_o, xc, mc[2], **cx)
        else:
            jx = i // N_MIXERS
            w_in = (sconv_in_b, jx)
            w_out = (sconv_out_b, jx)
            z_l = _sconv_in_call(xl, ml[0], ml[1], w_in, sconv_conv[jx], **lat_seq)
            xl = _proj_res_call(z_l, w_out, xl, ml[2], **lat)
            if ctx_update:
                z_c = _sconv_in_call(xc, mc[0], mc[1], w_in, sconv_conv[jx], **cx_seq)
                xc = _proj_res_call(z_c, w_out, xc, mc[2], **cx)
        u_l = _ffn_up_call(xl, ml[3], ml[4], w_up, ffn_conv[i], **lat_seq)
        xl = _proj_res_call(u_l, w_down, xl, ml[5], **lat)
        if ctx_update:
            u_c = _ffn_up_call(xc, mc[3], mc[4], w_up, ffn_conv[i], **cx_seq)
            xc = _proj_res_call(u_c, w_down, xc, mc[5], **cx)
    return _from_tiles(_final_norm_call(xl, final_gain, tm=tm_l), tm_l).reshape(b, t, d)
```

```python
import functools
import math

import jax
import jax.numpy as jnp
from jax import lax
from jax.experimental import pallas as pl
from jax.experimental.pallas import tpu as pltpu

F32 = jnp.float32
BF16 = jnp.bfloat16

GRID_W = 64
N_MIXERS = 2
DIFF_HEADS = 8
HEAD_DIM = 128
V_HEAD_DIM = 2 * HEAD_DIM
QK_WIDTH = DIFF_HEADS * 2 * HEAD_DIM
V_WIDTH = DIFF_HEADS * V_HEAD_DIM
ROPE_THETA = 10000.0
N_MOD = 6
EPS = 1e-6

LANES = 128
SUBLANES = 8
HALO = 16
V7X_VMEM_LIMIT = 56 * 1024 * 1024
ADA_ROWS = 8
TN = 512
Q_SCALE = HEAD_DIM ** -0.5 * math.log2(math.e)


def _pick(n, candidates):
    for c in candidates:
        if n % c == 0:
            return c
    return n


def _params(*sem):
    return pltpu.CompilerParams(dimension_semantics=sem, vmem_limit_bytes=V7X_VMEM_LIMIT)


def _norm_mod(x, shift, scale):
    ms = jnp.mean(x * x, axis=-1, keepdims=True)
    return (x * lax.rsqrt(ms + EPS)) * (1.0 + scale) + shift


def _ada_kernel(c_ref, w_ref, b_ref, o_ref):
    c = c_ref[...]
    s = (c * jax.nn.sigmoid(c)).astype(BF16)
    o_ref[0] = jnp.dot(s, w_ref[0].astype(BF16), preferred_element_type=F32) + b_ref[0]


def _ada_call(cond, w_ada, b_ada):
    depth, d, n = w_ada.shape
    tn = _pick(n, (1024, 512, 256, 128))
    return pl.pallas_call(
        _ada_kernel,
        grid=(depth, n // tn),
        in_specs=[
            pl.BlockSpec((ADA_ROWS, d), lambda l, j: (0, 0)),
            pl.BlockSpec((1, d, tn), lambda l, j: (l, 0, j)),
            pl.BlockSpec((1, 1, tn), lambda l, j: (l, 0, j)),
        ],
        out_specs=pl.BlockSpec((1, ADA_ROWS, tn), lambda l, j: (l, 0, j)),
        out_shape=jax.ShapeDtypeStruct((depth, ADA_ROWS, n), F32),
        compiler_params=_params("arbitrary", "arbitrary"),
        name="ada",
    )(cond, w_ada, b_ada.reshape(depth, 1, n))


def _rope(y, cos, sin_lo, sin_hi):
    return y * cos + pltpu.roll(y, LANES - 32, 1) * sin_lo + pltpu.roll(y, 32, 1) * sin_hi


def _qkv_kernel(x_ref, sh_ref, sc_ref, cos_ref, slo_ref, shi_ref, *rest, tn, with_q, rope):
    if with_q:
        wq_ref, wk_ref, wv_ref, qt_ref, k_ref, vt_ref, h_scr = rest
    else:
        wk_ref, wv_ref, k_ref, vt_ref, h_scr = rest

    @pl.when(pl.program_id(1) == 0)
    def _():
        h_scr[...] = _norm_mod(x_ref[...], sh_ref[0], sc_ref[0]).astype(BF16)

    h = h_scr[...]

    def groups(y):
        for g in range(tn // LANES):
            yg = y[:, g * LANES:(g + 1) * LANES]
            if rope:
                yg = _rope(yg, cos_ref[...], slo_ref[...], shi_ref[...])
            yield g, yg

    if with_q:
        for g, yg in groups(jnp.dot(h, wq_ref[...], preferred_element_type=F32)):
            qt_ref[g * LANES:(g + 1) * LANES, :] = (yg * Q_SCALE).T.astype(BF16)
    for g, yg in groups(jnp.dot(h, wk_ref[...], preferred_element_type=F32)):
        k_ref[:, g * LANES:(g + 1) * LANES] = yg.astype(BF16)
    vt_ref[...] = jnp.dot(h, wv_ref[...], preferred_element_type=F32).T.astype(BF16)


def _qkv_call(x, shift, scale, w_layer, tables, *, n_seq, tm, tiles_per_mod, with_q, rope):
    m, d = x.shape
    tn = TN
    assert QK_WIDTH == V_WIDTH
    seq = m // n_seq
    tps = seq // tm
    nt = QK_WIDTH // tn
    cos, slo, shi = tables
    pos_tiles = max(cos.shape[0] // tm, 1)
    tbl = pl.BlockSpec((tm, LANES), lambda i, j: (i % pos_tiles, 0))
    mod = pl.BlockSpec((1, 1, d), lambda i, j: (i // tiles_per_mod, 0, 0))
    wspec = lambda off: _wspec(w_layer, d, tn, off)
    w_specs = [wspec(nt), wspec(2 * nt)]
    out_specs = [
        pl.BlockSpec((tm, tn), lambda i, j: (i, j)),
        pl.BlockSpec((None, None, tn, tm), lambda i, j: (i // tps, i % tps, j, 0)),
    ]
    out_shape = [
        jax.ShapeDtypeStruct((m, QK_WIDTH), BF16),
        jax.ShapeDtypeStruct((n_seq, tps, V_WIDTH, tm), BF16),
    ]
    if with_q:
        w_specs.insert(0, wspec(0))
        out_specs.insert(0, pl.BlockSpec((None, tn, tm), lambda i, j: (i // tps, j, i % tps)))
        out_shape.insert(0, jax.ShapeDtypeStruct((n_seq, QK_WIDTH, seq), BF16))
    return pl.pallas_call(
        functools.partial(_qkv_kernel, tn=tn, with_q=with_q, rope=rope),
        grid=(m // tm, nt),
        in_specs=[pl.BlockSpec((tm, d), lambda i, j: (i, 0)), mod, mod, tbl, tbl, tbl] + w_specs,
        out_specs=out_specs,
        out_shape=out_shape,
        scratch_shapes=[pltpu.VMEM((tm, d), BF16)],
        compiler_params=_params("parallel", "arbitrary"),
        name="qkv",
    )(x, shift, scale, cos, slo, shi, *([w_layer[0]] * len(w_specs)))


def _attn_kernel(lq1_ref, lk1_ref, lq2_ref, lk2_ref, gain_ref, qt_ref, *rest, chunks, lam_init):
    n_src = len(chunks)
    srcs = [(rest[2 * s], rest[2 * s + 1]) for s in range(n_src)]
    o_ref, acc0, acc1, sa0, sa1, sb0, sb1 = rest[2 * n_src:]
    qt = qt_ref[...]
    q0t = qt[:HEAD_DIM, :]
    q1t = qt[HEAD_DIM:, :]
    tq = qt.shape[1]
    acc0[...] = jnp.zeros_like(acc0)
    acc1[...] = jnp.zeros_like(acc1)

    def scores(k):
        return (jnp.dot(k[:, :HEAD_DIM], q0t, preferred_element_type=F32),
                jnp.dot(k[:, HEAD_DIM:], q1t, preferred_element_type=F32))

    def update(s, mc, vt, acc, m, l):
        m_new = jnp.maximum(m, mc)
        alpha = jnp.exp2(m - m_new)
        p = jnp.exp2(s - m_new)
        l_new = alpha * l + jnp.sum(p, axis=0, keepdims=True)
        acc[...] = alpha * acc[...] + jnp.dot(vt, p.astype(BF16), preferred_element_type=F32)
        return m_new, l_new

    def colmax(s):
        return jnp.max(s, axis=0, keepdims=True)

    def step(k, vt, carry):
        m0, l0, m1, l1 = carry
        s0, s1 = scores(k)
        m0, l0 = update(s0, colmax(s0), vt, acc0, m0, l0)
        m1, l1 = update(s1, colmax(s1), vt, acc1, m1, l1)
        return m0, l0, m1, l1

    def stage(k, s0_scr, s1_scr):
        s0, s1 = scores(k)
        s0_scr[...] = s0
        s1_scr[...] = s1
        return colmax(s0), colmax(s1)

    def consume(s0_scr, s1_scr, mc, vt, carry):
        m0, l0, m1, l1 = carry
        m0, l0 = update(s0_scr[...], mc[0], vt, acc0, m0, l0)
        m1, l1 = update(s1_scr[...], mc[1], vt, acc1, m1, l1)
        return m0, l0, m1, l1

    neg = jnp.full((1, tq), -jnp.inf, F32)
    zero = jnp.zeros((1, tq), F32)
    carry = (neg, zero, neg, zero)
    order = sorted(range(n_src), key=lambda s: chunks[s][0])
    lead = None
    for s in order:
        (k_ref, vt_ref), (n_chunks, tk) = srcs[s], chunks[s]
        chunk = lambda c, k_ref=k_ref, tk=tk: k_ref[pl.ds(pl.multiple_of(c * tk, tk), tk), :]
        if n_chunks == 1 and lead is None and s != order[-1]:
            lead = (k_ref[...], vt_ref[0])
        elif n_chunks == 1:
            carry = step(k_ref[...], vt_ref[0], carry)
        elif n_chunks % 2 == 0 and sa0.shape[0] == tk:
            def body(i, state, chunk=chunk, vt_ref=vt_ref):
                carry, mca = state[:4], state[4:]
                c = 2 * i
                mcb = stage(chunk(c + 1), sb0, sb1)
                carry = consume(sa0, sa1, mca, vt_ref[c], carry)
                mca = stage(chunk(c + 2), sa0, sa1)
                carry = consume(sb0, sb1, mcb, vt_ref[c + 1], carry)
                return carry + mca
            if lead is not None:
                s0, s1 = scores(lead[0])
            mca = stage(chunk(0), sa0, sa1)
            if lead is not None:
                m0, l0, m1, l1 = carry
                m0, l0 = update(s0, colmax(s0), lead[1], acc0, m0, l0)
                m1, l1 = update(s1, colmax(s1), lead[1], acc1, m1, l1)
                carry, lead = (m0, l0, m1, l1), None
            state = lax.fori_loop(0, n_chunks // 2 - 1, body, carry + mca)
            carry, mca = state[:4], state[4:]
            mcb = stage(chunk(n_chunks - 1), sb0, sb1)
            carry = consume(sa0, sa1, mca, vt_ref[n_chunks - 2], carry)
            carry = consume(sb0, sb1, mcb, vt_ref[n_chunks - 1], carry)
        else:
            carry = lax.fori_loop(
                0, n_chunks, lambda c, carry, chunk=chunk, vt_ref=vt_ref: step(chunk(c), vt_ref[c], carry), carry)
    if lead is not None:
        carry = step(*lead, carry)
    _, l0, _, l1 = carry

    lam = (jnp.exp(jnp.sum(lq1_ref[...] * lk1_ref[...], axis=-1, keepdims=True))
           - jnp.exp(jnp.sum(lq2_ref[...] * lk2_ref[...], axis=-1, keepdims=True)) + lam_init)
    ot = acc0[...] / l0 - lam * (acc1[...] / l1)
    ms = jnp.mean(ot * ot, axis=0, keepdims=True)
    o = (ot * lax.rsqrt(ms + EPS)).T * gain_ref[...] * (1.0 - lam_init)
    o_ref[...] = o.astype(o_ref.dtype)


def _attn_call(qt, srcs, lams, gain, *, lam_init):
    n_seq, _, n_q = qt.shape
    tq = _pick(n_q, (512, 256, 128))
    nq = n_q // tq
    hw = V_HEAD_DIM
    lam_spec = pl.BlockSpec((1, HEAD_DIM), lambda b, h, i: (0, 0))
    in_specs = [
        lam_spec, lam_spec, lam_spec, lam_spec,
        pl.BlockSpec((1, hw), lambda b, h, i: (0, 0)),
        pl.BlockSpec((None, hw, tq), lambda b, h, i: (b, h, i)),
    ]
    args, chunks = [], []
    for k, vt in srcs:
        n_chunks, tk = vt.shape[1], vt.shape[3]
        chunks.append((n_chunks, tk))
        in_specs.append(pl.BlockSpec((None, k.shape[1], hw), lambda b, h, i: (b, 0, h)))
        in_specs.append(pl.BlockSpec((None, n_chunks, hw, tk), lambda b, h, i: (b, 0, h, 0)))
        args += [k, vt]
    stage_tk = max(chunks)[1] if max(chunks)[0] > 1 else 8
    s_scr = pltpu.VMEM((stage_tk, tq), F32)
    return pl.pallas_call(
        functools.partial(_attn_kernel, chunks=tuple(chunks), lam_init=lam_init),
        grid=(n_seq, DIFF_HEADS, nq),
        in_specs=in_specs,
        out_specs=pl.BlockSpec((tq, hw), lambda b, h, i: (b * nq + i, h)),
        out_shape=jax.ShapeDtypeStruct((n_seq * n_q, V_WIDTH), BF16),
        scratch_shapes=[pltpu.VMEM((hw, tq), F32), pltpu.VMEM((hw, tq), F32), s_scr, s_scr, s_scr, s_scr],
        compiler_params=_params("parallel", "parallel", "arbitrary"),
        name="diff_attn",
    )(*lams, gain, qt, *args)


def _to_tiles(x, tm):
    m, d = x.shape
    return x.reshape(m // tm, SUBLANES, tm // SUBLANES, d).swapaxes(1, 2).reshape(m, d)


def _from_tiles(x, tm):
    m, d = x.shape
    return x.reshape(m // tm, tm // SUBLANES, SUBLANES, d).swapaxes(1, 2).reshape(m, d)


def _fill_h(h_scr, x_ref, xp_ref, xn_ref, sh, sc, first, last, tm):
    h = _norm_mod(x_ref[...], sh, sc)
    prev_tok = jnp.where(first, 0.0, _norm_mod(xp_ref[...], sh, sc)[HALO - 1:HALO, :])
    next_tok = jnp.where(last, 0.0, _norm_mod(xn_ref[...], sh, sc)[0:1, :])
    row = lax.broadcasted_iota(jnp.int32, (SUBLANES, h.shape[1]), 0)
    up = jnp.where(row == 0, prev_tok, pltpu.roll(h[tm - SUBLANES:, :], 1, 0))
    down = jnp.where(row == SUBLANES - 1, next_tok, pltpu.roll(h[:SUBLANES, :], SUBLANES - 1, 0))
    zeros = jnp.zeros_like(up)
    h_scr[0:HALO, :] = jnp.concatenate([zeros, up], axis=0).astype(BF16)
    h_scr[HALO:HALO + tm, :] = h.astype(BF16)
    h_scr[HALO + tm:, :] = jnp.concatenate([down, zeros], axis=0).astype(BF16)


def _conv3(y_scr, w_ref, tm):
    return (y_scr[pl.ds(HALO - SUBLANES, tm), :] * w_ref[0:1, :]
            + y_scr[pl.ds(HALO, tm), :] * w_ref[1:2, :]
            + y_scr[pl.ds(HALO + SUBLANES, tm), :] * w_ref[2:3, :])


def _halo_specs(tm, d, m):
    r = tm // HALO
    last_blk = m // HALO - 1
    return [
        pl.BlockSpec((tm, d), lambda i, j: (i, 0)),
        pl.BlockSpec((HALO, d), lambda i, j: (jnp.maximum(i * r - 1, 0), 0)),
        pl.BlockSpec((HALO, d), lambda i, j: (jnp.minimum((i + 1) * r, last_blk), 0)),
    ]


def _conv_prologue(x_ref, xp_ref, xn_ref, sh_ref, sc_ref, h_scr, *, tm, tiles_per_seq):
    @pl.when(pl.program_id(1) == 0)
    def _():
        t = pl.program_id(0) % tiles_per_seq
        _fill_h(h_scr, x_ref, xp_ref, xn_ref, sh_ref[0], sc_ref[0], t == 0, t == tiles_per_seq - 1, tm)


def _wspec(w_layer, d, tn, off):
    layer = w_layer[1]
    return pl.BlockSpec((None, d, tn), lambda i, j: (layer, 0, j + off))


def _sconv_in_kernel(x_ref, xp_ref, xn_ref, sh_ref, sc_ref, wb_ref, wc_ref, wx_ref, cw_ref, o_ref,
                     h_scr, p_scr, *, tm, tiles_per_seq):
    _conv_prologue(x_ref, xp_ref, xn_ref, sh_ref, sc_ref, h_scr, tm=tm, tiles_per_seq=tiles_per_seq)
    h = h_scr[...]
    p_scr[...] = (jnp.dot(h, wc_ref[...], preferred_element_type=F32)
                  * jnp.dot(h, wx_ref[...], preferred_element_type=F32))
    gb = jnp.dot(h_scr[HALO:HALO + tm, :], wb_ref[...], preferred_element_type=F32)
    o_ref[...] = (gb * _conv3(p_scr, cw_ref, tm)).astype(o_ref.dtype)


def _sconv_in_call(x, shift, scale, w_layer, conv_w, *, tm, tiles_per_mod, tiles_per_seq):
    m, d = x.shape
    w = w_layer[0]
    tn = TN
    nt = w.shape[2] // 3 // tn
    mod = pl.BlockSpec((1, 1, d), lambda i, j: (i // tiles_per_mod, 0, 0))
    ext = tm + 2 * HALO
    return pl.pallas_call(
        functools.partial(_sconv_in_kernel, tm=tm, tiles_per_seq=tiles_per_seq),
        grid=(m // tm, nt),
        in_specs=_halo_specs(tm, d, m) + [
            mod, mod, _wspec(w_layer, d, tn, 0), _wspec(w_layer, d, tn, nt), _wspec(w_layer, d, tn, 2 * nt),
            pl.BlockSpec((3, tn), lambda i, j: (0, j)),
        ],
        out_specs=pl.BlockSpec((tm, tn), lambda i, j: (i, j)),
        out_shape=jax.ShapeDtypeStruct((m, nt * tn), BF16),
        scratch_shapes=[pltpu.VMEM((ext, d), BF16), pltpu.VMEM((ext, tn), F32)],
        compiler_params=_params("parallel", "arbitrary"),
        name="sconv_in",
    )(x, x, x, shift, scale, w, w, w, conv_w)


def _ffn_up_kernel(x_hbm, xp_ref, xn_ref, sh_ref, sc_ref, wg_ref, wv_ref, cg_ref, cv_ref, o_ref,
                   h_scr, yg_scr, yv_scr, x_scr, sem, *, tm, tiles_per_seq):
    i = pl.program_id(0)
    slot = i % 2

    def fetch(row, dst):
        rows = pl.ds(pl.multiple_of(row * tm, tm), tm)
        return pltpu.make_async_copy(x_hbm.at[rows, :], x_scr.at[dst], sem.at[dst])

    @pl.when(pl.program_id(1) == 0)
    def _():
        @pl.when(i == 0)
        def _():
            fetch(0, 0).start()

        fetch(i, slot).wait()

        @pl.when(i + 1 < pl.num_programs(0))
        def _():
            fetch(i + 1, 1 - slot).start()

    _conv_prologue(x_scr.at[slot], xp_ref, xn_ref, sh_ref, sc_ref, h_scr, tm=tm, tiles_per_seq=tiles_per_seq)
    h = h_scr[...]
    yg_scr[...] = jnp.dot(h, wg_ref[...], preferred_element_type=F32)
    yv_scr[...] = jnp.dot(h, wv_ref[...], preferred_element_type=F32)
    ug = _conv3(yg_scr, cg_ref, tm)
    uv = _conv3(yv_scr, cv_ref, tm)
    o_ref[...] = (ug * jax.nn.sigmoid(ug) * uv).astype(o_ref.dtype)


def _ffn_up_call(x, shift, scale, w_layer, conv_w, *, tm, tiles_per_mod, tiles_per_seq):
    m, d = x.shape
    w = w_layer[0]
    tn = TN
    nt = w.shape[2] // 2 // tn
    mod = pl.BlockSpec((1, 1, d), lambda i, j: (i // tiles_per_mod, 0, 0))
    cspec = lambda off: pl.BlockSpec((3, tn), lambda i, j: (0, j + off))
    ext = tm + 2 * HALO
    return pl.pallas_call(
        functools.partial(_ffn_up_kernel, tm=tm, tiles_per_seq=tiles_per_seq),
        grid=(m // tm, nt),
        in_specs=[pl.BlockSpec(memory_space=pl.ANY)] + _halo_specs(tm, d, m)[1:] + [
            mod, mod, _wspec(w_layer, d, tn, 0), _wspec(w_layer, d, tn, nt), cspec(0), cspec(nt)],
        out_specs=pl.BlockSpec((tm, tn), lambda i, j: (i, j)),
        out_shape=jax.ShapeDtypeStruct((m, nt * tn), BF16),
        scratch_shapes=[pltpu.VMEM((ext, d), BF16), pltpu.VMEM((ext, tn), F32), pltpu.VMEM((ext, tn), F32),
                        pltpu.VMEM((2, tm, d), F32), pltpu.SemaphoreType.DMA((2,))],
        compiler_params=_params("arbitrary", "arbitrary"),
        name="ffn_up",
    )(x, x, x, shift, scale, w, w, conv_w, conv_w)


def _proj_res_kernel(a_ref, w_ref, res_ref, g_ref, o_ref):
    y = jnp.dot(a_ref[...], w_ref[...], preferred_element_type=F32)
    o_ref[...] = res_ref[...] + g_ref[0] * y


def _proj_res_stream_kernel(a_hbm, w_ref, res_ref, g_ref, o_ref, a_scr, sem, *, tm):
    i = pl.program_id(0)
    slot = i % 2

    def fetch(row, dst):
        rows = pl.ds(pl.multiple_of(row * tm, tm), tm)
        return pltpu.make_async_copy(a_hbm.at[rows, :], a_scr.at[dst], sem.at[dst])

    @pl.when(pl.program_id(1) == 0)
    def _():
        @pl.when(i == 0)
        def _():
            fetch(0, 0).start()

        fetch(i, slot).wait()

        @pl.when(i + 1 < pl.num_programs(0))
        def _():
            fetch(i + 1, 1 - slot).start()

    y = jnp.dot(a_scr[slot], w_ref[...], preferred_element_type=F32)
    o_ref[...] = res_ref[...] + g_ref[0] * y


def _proj_res_call(a, w_layer, res, gate, *, tm, tiles_per_mod):
    m, k = a.shape
    w = w_layer[0]
    n = w.shape[2]
    tn = n if k * n * 2 <= 8 * 1024 * 1024 else TN
    if n // tn > 1:
        return pl.pallas_call(
            functools.partial(_proj_res_stream_kernel, tm=tm),
            grid=(m // tm, n // tn),
            in_specs=[
                pl.BlockSpec(memory_space=pl.ANY),
                _wspec(w_layer, k, tn, 0),
                pl.BlockSpec((tm, tn), lambda i, j: (i, j)),
                pl.BlockSpec((1, 1, tn), lambda i, j: (i // tiles_per_mod, 0, j)),
            ],
            out_specs=pl.BlockSpec((tm, tn), lambda i, j: (i, j)),
            out_shape=jax.ShapeDtypeStruct((m, n), F32),
            scratch_shapes=[pltpu.VMEM((2, tm, k), BF16), pltpu.SemaphoreType.DMA((2,))],
            compiler_params=_params("arbitrary", "arbitrary"),
            name="proj_res_stream",
        )(a, w, res, gate)
    return pl.pallas_call(
        _proj_res_kernel,
        grid=(m // tm, n // tn),
        in_specs=[
            pl.BlockSpec((tm, k), lambda i, j: (i, 0)),
            _wspec(w_layer, k, tn, 0),
            pl.BlockSpec((tm, tn), lambda i, j: (i, j)),
            pl.BlockSpec((1, 1, tn), lambda i, j: (i // tiles_per_mod, 0, j)),
        ],
        out_specs=pl.BlockSpec((tm, tn), lambda i, j: (i, j)),
        out_shape=jax.ShapeDtypeStruct((m, n), F32),
        compiler_params=_params("parallel", "arbitrary"),
        name="proj_res",
    )(a, w, res, gate)


def _final_norm_kernel(x_ref, g_ref, o_ref):
    x = x_ref[...]
    ms = jnp.mean(x * x, axis=-1, keepdims=True)
    o_ref[...] = (x * lax.rsqrt(ms + EPS)) * g_ref[...]


def _final_norm_call(x, gain, *, tm):
    m, d = x.shape
    return pl.pallas_call(
        _final_norm_kernel,
        grid=(m // tm,),
        in_specs=[pl.BlockSpec((tm, d), lambda i: (i, 0)), pl.BlockSpec((1, d), lambda i: (0, 0))],
        out_specs=pl.BlockSpec((tm, d), lambda i: (i, 0)),
        out_shape=jax.ShapeDtypeStruct((m, d), F32),
        compiler_params=_params("parallel"),
        name="final_norm",
    )(x, gain.reshape(1, d))


def _rope_tables(n_tokens):
    rows = n_tokens // GRID_W
    row = jnp.repeat(jnp.arange(rows, dtype=F32), GRID_W)
    col = jnp.tile(jnp.arange(GRID_W, dtype=F32), rows)
    half = HEAD_DIM // 2
    inv_freq = 1.0 / (ROPE_THETA ** (jnp.arange(0, half, 2, dtype=F32) / half))
    ang_r = row[:, None] * inv_freq
    ang_c = col[:, None] * inv_freq
    zeros = jnp.zeros_like(ang_r)
    cos = jnp.concatenate([jnp.cos(ang_r)] * 2 + [jnp.cos(ang_c)] * 2, axis=-1)
    sin_lo = jnp.concatenate([-jnp.sin(ang_r), zeros, -jnp.sin(ang_c), zeros], axis=-1)
    sin_hi = jnp.concatenate([zeros, jnp.sin(ang_r), zeros, jnp.sin(ang_c)], axis=-1)
    return cos, sin_lo, sin_hi


def kernel(x, c, ctx, c_ctx, w_ada, b_ada, attn_w_qkv, attn_w_o, attn_lambda_q1, attn_lambda_k1,
           attn_lambda_q2, attn_lambda_k2, attn_subln_gain, sconv_w_in, sconv_conv, sconv_w_out,
           ffn_w_up, ffn_conv, ffn_w_down, final_gain):
    b, t, d = x.shape
    lc = ctx.shape[1]
    depth = w_ada.shape[0]
    assert b + 1 <= ADA_ROWS and t % GRID_W == 0
    tm_l = _pick(t, (1024, 512, 256, 128))
    tm_c = _pick(lc, (256, 128))
    lat = dict(tm=tm_l, tiles_per_mod=t // tm_l)
    cx = dict(tm=tm_c, tiles_per_mod=(b * lc) // tm_c)
    lat_seq = dict(tiles_per_seq=t // tm_l, **lat)
    cx_seq = dict(tiles_per_seq=lc // tm_c, **cx)

    cond = jnp.zeros((ADA_ROWS, d), F32).at[:b].set(c).at[b].set(c_ctx)
    mods = _ada_call(cond, w_ada, b_ada).reshape(depth, ADA_ROWS, N_MOD, d)
    tables = [_to_tiles(tbl, tm_l) for tbl in _rope_tables(t)]
    ffn_up_b, ffn_down_b = ffn_w_up.astype(BF16), ffn_w_down.astype(BF16)
    attn_qkv_b, attn_o_b = attn_w_qkv.astype(BF16), attn_w_o.astype(BF16)
    sconv_in_b, sconv_out_b = sconv_w_in.astype(BF16), sconv_w_out.astype(BF16)
    last_attn = max(i for i in range(depth) if i % N_MIXERS == 0)

    xl = _to_tiles(x.reshape(b * t, d), tm_l)
    xc = _to_tiles(ctx.reshape(b * lc, d), tm_c)
    for i in range(depth):
        ctx_update = i < last_attn
        is_attn = i % N_MIXERS == 0
        ctx_read = i <= last_attn and is_attn
        ml = [mods[i, :b, k].reshape(b, 1, d) for k in range(N_MOD)]
        mc = [mods[i, b:b + 1, k].reshape(1, 1, d) for k in range(N_MOD)]
        w_up = (ffn_up_b, i)
        w_down = (ffn_down_b, i)
        if is_attn:
            a = i // N_MIXERS
            lam_init = 0.8 - 0.6 * math.exp(-0.3 * i)
            w_qkv = (attn_qkv_b, a)
            w_o = (attn_o_b, a)
            lams = [p[a].reshape(1, HEAD_DIM) for p in
                    (attn_lambda_q1, attn_lambda_k1, attn_lambda_q2, attn_lambda_k2)]
            gain = attn_subln_gain[a].reshape(1, V_HEAD_DIM)
            qt_l, k_l, vt_l = _qkv_call(xl, ml[0], ml[1], w_qkv, tables, n_seq=b, with_q=True, rope=True, **lat)
            srcs = [(k_l.reshape(b, t, QK_WIDTH), vt_l)]
            if ctx_read:
                outs_c = _qkv_call(xc, mc[0], mc[1], w_qkv, tables, n_seq=b, with_q=ctx_update, rope=False, **cx)
                k_c, vt_c = outs_c[-2].reshape(b, lc, QK_WIDTH), outs_c[-1]
                srcs.append((k_c, vt_c))
            o_l = _attn_call(qt_l, srcs, lams, gain, lam_init=lam_init)
            xl = _proj_res_call(o_l, w_o, xl, ml[2], **lat)
            if ctx_update:
                o_c = _attn_call(outs_c[0], [(k_c, vt_c)], lams, gain, lam_init=lam_init)
                xc = _proj_res_call(o_c, w_o, xc, mc[2], **cx)
        else:
            jx = i // N_MIXERS
            w_in = (sconv_in_b, jx)
            w_out = (sconv_out_b, jx)
            z_l = _sconv_in_call(xl, ml[0], ml[1], w_in, sconv_conv[jx], **lat_seq)
            xl = _proj_res_call(z_l, w_out, xl, ml[2], **lat)
            if ctx_update:
                z_c = _sconv_in_call(xc, mc[0], mc[1], w_in, sconv_conv[jx], **cx_seq)
                xc = _proj_res_call(z_c, w_out, xc, mc[2], **cx)
        u_l = _ffn_up_call(xl, ml[3], ml[4], w_up, ffn_conv[i], **lat_seq)
        xl = _proj_res_call(u_l, w_down, xl, ml[5], **lat)
        if ctx_update:
            u_c = _ffn_up_call(xc, mc[3], mc[4], w_up, ffn_conv[i], **cx_seq)
            xc = _proj_res_call(u_c, w_down, xc, mc[5], **cx)
    return _from_tiles(_final_norm_call(xl, final_gain, tm=tm_l), tm_l).reshape(b, t, d)
```

```python
import functools
import math

import jax
import jax.numpy as jnp
from jax import lax
from jax.experimental import pallas as pl
from jax.experimental.pallas import tpu as pltpu

F32 = jnp.float32
BF16 = jnp.bfloat16

GRID_W = 64
N_MIXERS = 2
DIFF_HEADS = 8
HEAD_DIM = 128
V_HEAD_DIM = 2 * HEAD_DIM
QK_WIDTH = DIFF_HEADS * 2 * HEAD_DIM
V_WIDTH = DIFF_HEADS * V_HEAD_DIM
ROPE_THETA = 10000.0
N_MOD = 6
EPS = 1e-6

LANES = 128
SUBLANES = 8
HALO = 16
V7X_VMEM_LIMIT = 56 * 1024 * 1024
ADA_ROWS = 8
TN = 512
Q_SCALE = HEAD_DIM ** -0.5 * math.log2(math.e)


def _pick(n, candidates):
    for c in candidates:
        if n % c == 0:
            return c
    return n


def _params(*sem):
    return pltpu.CompilerParams(dimension_semantics=sem, vmem_limit_bytes=V7X_VMEM_LIMIT)


def _norm_mod(x, shift, scale):
    ms = jnp.mean(x * x, axis=-1, keepdims=True)
    return (x * lax.rsqrt(ms + EPS)) * (1.0 + scale) + shift


def _ada_kernel(c_ref, w_ref, b_ref, o_ref):
    c = c_ref[...]
    s = (c * jax.nn.sigmoid(c)).astype(BF16)
    o_ref[0] = jnp.dot(s, w_ref[0].astype(BF16), preferred_element_type=F32) + b_ref[0]


def _ada_call(cond, w_ada, b_ada):
    depth, d, n = w_ada.shape
    tn = _pick(n, (1024, 512, 256, 128))
    return pl.pallas_call(
        _ada_kernel,
        grid=(depth, n // tn),
        in_specs=[
            pl.BlockSpec((ADA_ROWS, d), lambda l, j: (0, 0)),
            pl.BlockSpec((1, d, tn), lambda l, j: (l, 0, j)),
            pl.BlockSpec((1, 1, tn), lambda l, j: (l, 0, j)),
        ],
        out_specs=pl.BlockSpec((1, ADA_ROWS, tn), lambda l, j: (l, 0, j)),
        out_shape=jax.ShapeDtypeStruct((depth, ADA_ROWS, n), F32),
        compiler_params=_params("arbitrary", "arbitrary"),
        name="ada",
    )(cond, w_ada, b_ada.reshape(depth, 1, n))


def _rope(y, cos, sin_lo, sin_hi):
    return y * cos + pltpu.roll(y, LANES - 32, 1) * sin_lo + pltpu.roll(y, 32, 1) * sin_hi


def _qkv_kernel(x_ref, sh_ref, sc_ref, cos_ref, slo_ref, shi_ref, *rest, tn, with_q, rope):
    if with_q:
        wq_ref, wk_ref, wv_ref, qt_ref, k_ref, vt_ref, h_scr = rest
    else:
        wk_ref, wv_ref, k_ref, vt_ref, h_scr = rest

    @pl.when(pl.program_id(1) == 0)
    def _():
        h_scr[...] = _norm_mod(x_ref[...], sh_ref[0], sc_ref[0]).astype(BF16)

    h = h_scr[...]

    def groups(y):
        for g in range(tn // LANES):
            yg = y[:, g * LANES:(g + 1) * LANES]
            if rope:
                yg = _rope(yg, cos_ref[...], slo_ref[...], shi_ref[...])
            yield g, yg

    if with_q:
        for g, yg in groups(jnp.dot(h, wq_ref[...], preferred_element_type=F32)):
            qt_ref[g * LANES:(g + 1) * LANES, :] = (yg * Q_SCALE).T.astype(BF16)
    for g, yg in groups(jnp.dot(h, wk_ref[...], preferred_element_type=F32)):
        k_ref[:, g * LANES:(g + 1) * LANES] = yg.astype(BF16)
    vt_ref[...] = jnp.dot(h, wv_ref[...], preferred_element_type=F32).T.astype(BF16)


def _qkv_call(x, shift, scale, w_layer, tables, *, n_seq, tm, tiles_per_mod, with_q, rope):
    m, d = x.shape
    tn = TN
    assert QK_WIDTH == V_WIDTH
    seq = m // n_seq
    tps = seq // tm
    nt = QK_WIDTH // tn
    cos, slo, shi = tables
    pos_tiles = max(cos.shape[0] // tm, 1)
    tbl = pl.BlockSpec((tm, LANES), lambda i, j: (i % pos_tiles, 0))
    mod = pl.BlockSpec((1, 1, d), lambda i, j: (i // tiles_per_mod, 0, 0))
    wspec = lambda off: _wspec(w_layer, d, tn, off)
    w_specs = [wspec(nt), wspec(2 * nt)]
    out_specs = [
        pl.BlockSpec((tm, tn), lambda i, j: (i, j)),
        pl.BlockSpec((None, None, tn, tm), lambda i, j: (i // tps, i % tps, j, 0)),
    ]
    out_shape = [
        jax.ShapeDtypeStruct((m, QK_WIDTH), BF16),
        jax.ShapeDtypeStruct((n_seq, tps, V_WIDTH, tm), BF16),
    ]
    if with_q:
        w_specs.insert(0, wspec(0))
        out_specs.insert(0, pl.BlockSpec((None, tn, tm), lambda i, j: (i // tps, j, i % tps)))
        out_shape.insert(0, jax.ShapeDtypeStruct((n_seq, QK_WIDTH, seq), BF16))
    return pl.pallas_call(
        functools.partial(_qkv_kernel, tn=tn, with_q=with_q, rope=rope),
        grid=(m // tm, nt),
        in_specs=[pl.BlockSpec((tm, d), lambda i, j: (i, 0)), mod, mod, tbl, tbl, tbl] + w_specs,
        out_specs=out_specs,
        out_shape=out_shape,
        scratch_shapes=[pltpu.VMEM((tm, d), BF16)],
        compiler_params=_params("parallel", "arbitrary"),
        name="qkv",
    )(x, shift, scale, cos, slo, shi, *([w_layer[0]] * len(w_specs)))


def _attn_kernel(lq1_ref, lk1_ref, lq2_ref, lk2_ref, gain_ref, qt_ref, *rest, chunks, lam_init):
    n_src = len(chunks)
    srcs = [(rest[2 * s], rest[2 * s + 1]) for s in range(n_src)]
    o_ref, acc0, acc1, sa0, sa1, sb0, sb1 = rest[2 * n_src:]
    qt = qt_ref[...]
    q0t = qt[:HEAD_DIM, :]
    q1t = qt[HEAD_DIM:, :]
    tq = qt.shape[1]
    acc0[...] = jnp.zeros_like(acc0)
    acc1[...] = jnp.zeros_like(acc1)

    def scores(k):
        return (jnp.dot(k[:, :HEAD_DIM], q0t, preferred_element_type=F32),
                jnp.dot(k[:, HEAD_DIM:], q1t, preferred_element_type=F32))

    def update(s, mc, vt, acc, m, l):
        m_new = jnp.maximum(m, mc)
        alpha = jnp.exp2(m - m_new)
        p = jnp.exp2(s - m_new)
        l_new = alpha * l + jnp.sum(p, axis=0, keepdims=True)
        acc[...] = alpha * acc[...] + jnp.dot(vt, p.astype(BF16), preferred_element_type=F32)
        return m_new, l_new

    def colmax(s):
        return jnp.max(s, axis=0, keepdims=True)

    def step(k, vt, carry):
        m0, l0, m1, l1 = carry
        s0, s1 = scores(k)
        m0, l0 = update(s0, colmax(s0), vt, acc0, m0, l0)
        m1, l1 = update(s1, colmax(s1), vt, acc1, m1, l1)
        return m0, l0, m1, l1

    def stage(k, s0_scr, s1_scr):
        s0, s1 = scores(k)
        s0_scr[...] = s0
        s1_scr[...] = s1
        return colmax(s0), colmax(s1)

    def consume(s0_scr, s1_scr, mc, vt, carry):
        m0, l0, m1, l1 = carry
        m0, l0 = update(s0_scr[...], mc[0], vt, acc0, m0, l0)
        m1, l1 = update(s1_scr[...], mc[1], vt, acc1, m1, l1)
        return m0, l0, m1, l1

    neg = jnp.full((1, tq), -jnp.inf, F32)
    zero = jnp.zeros((1, tq), F32)
    carry = (neg, zero, neg, zero)
    order = sorted(range(n_src), key=lambda s: chunks[s][0])
    lead = None
    for s in order:
        (k_ref, vt_ref), (n_chunks, tk) = srcs[s], chunks[s]
        chunk = lambda c, k_ref=k_ref, tk=tk: k_ref[pl.ds(pl.multiple_of(c * tk, tk), tk), :]
        if n_chunks == 1 and lead is None and s != order[-1]:
            lead = (k_ref[...], vt_ref[0])
        elif n_chunks == 1:
            carry = step(k_ref[...], vt_ref[0], carry)
        elif n_chunks % 2 == 0 and sa0.shape[0] == tk:
            def body(i, state, chunk=chunk, vt_ref=vt_ref):
                carry, mca = state[:4], state[4:]
                c = 2 * i
                mcb = stage(chunk(c + 1), sb0, sb1)
                carry = consume(sa0, sa1, mca, vt_ref[c], carry)
                mca = stage(chunk(c + 2), sa0, sa1)
                carry = consume(sb0, sb1, mcb, vt_ref[c + 1], carry)
                return carry + mca
            if lead is not None:
                s0, s1 = scores(lead[0])
            mca = stage(chunk(0), sa0, sa1)
            if lead is not None:
                m0, l0, m1, l1 = carry
                m0, l0 = update(s0, colmax(s0), lead[1], acc0, m0, l0)
                m1, l1 = update(s1, colmax(s1), lead[1], acc1, m1, l1)
                carry, lead = (m0, l0, m1, l1), None
            state = lax.fori_loop(0, n_chunks // 2 - 1, body, carry + mca)
            carry, mca = state[:4], state[4:]
            mcb = stage(chunk(n_chunks - 1), sb0, sb1)
            carry = consume(sa0, sa1, mca, vt_ref[n_chunks - 2], carry)
            carry = consume(sb0, sb1, mcb, vt_ref[n_chunks - 1], carry)
        else:
            carry = lax.fori_loop(
                0, n_chunks, lambda c, carry, chunk=chunk, vt_ref=vt_ref: step(chunk(c), vt_ref[c], carry), carry)
    if lead is not None:
        carry = step(*lead, carry)
    _, l0, _, l1 = carry

    lam = (jnp.exp(jnp.sum(lq1_ref[...] * lk1_ref[...], axis=-1, keepdims=True))
           - jnp.exp(jnp.sum(lq2_ref[...] * lk2_ref[...], axis=-1, keepdims=True)) + lam_init)
    ot = acc0[...] / l0 - lam * (acc1[...] / l1)
    ms = jnp.mean(ot * ot, axis=0, keepdims=True)
    o = (ot * lax.rsqrt(ms + EPS)).T * gain_ref[...] * (1.0 - lam_init)
    o_ref[...] = o.astype(o_ref.dtype)


def _attn_call(qt, srcs, lams, gain, *, lam_init):
    n_seq, _, n_q = qt.shape
    tq = _pick(n_q, (512, 256, 128))
    nq = n_q // tq
    hw = V_HEAD_DIM
    lam_spec = pl.BlockSpec((1, HEAD_DIM), lambda b, h, i: (0, 0))
    in_specs = [
        lam_spec, lam_spec, lam_spec, lam_spec,
        pl.BlockSpec((1, hw), lambda b, h, i: (0, 0)),
        pl.BlockSpec((None, hw, tq), lambda b, h, i: (b, h, i)),
    ]
    args, chunks = [], []
    for k, vt in srcs:
        n_chunks, tk = vt.shape[1], vt.shape[3]
        chunks.append((n_chunks, tk))
        in_specs.append(pl.BlockSpec((None, k.shape[1], hw), lambda b, h, i: (b, 0, h)))
        in_specs.append(pl.BlockSpec((None, n_chunks, hw, tk), lambda b, h, i: (b, 0, h, 0)))
        args += [k, vt]
    stage_tk = max(chunks)[1] if max(chunks)[0] > 1 else 8
    s_scr = pltpu.VMEM((stage_tk, tq), F32)
    return pl.pallas_call(
        functools.partial(_attn_kernel, chunks=tuple(chunks), lam_init=lam_init),
        grid=(n_seq, DIFF_HEADS, nq),
        in_specs=in_specs,
        out_specs=pl.BlockSpec((tq, hw), lambda b, h, i: (b * nq + i, h)),
        out_shape=jax.ShapeDtypeStruct((n_seq * n_q, V_WIDTH), BF16),
        scratch_shapes=[pltpu.VMEM((hw, tq), F32), pltpu.VMEM((hw, tq), F32), s_scr, s_scr, s_scr, s_scr],
        compiler_params=_params("parallel", "parallel", "arbitrary"),
        name="diff_attn",
    )(*lams, gain, qt, *args)


def _to_tiles(x, tm):
    m, d = x.shape
    return x.reshape(m // tm, SUBLANES, tm // SUBLANES, d).swapaxes(1, 2).reshape(m, d)


def _from_tiles(x, tm):
    m, d = x.shape
    return x.reshape(m // tm, tm // SUBLANES, SUBLANES, d).swapaxes(1, 2).reshape(m, d)


def _fill_h(h_scr, x_ref, xp_ref, xn_ref, sh, sc, first, last, tm):
    h = _norm_mod(x_ref[...], sh, sc)
    prev_tok = jnp.where(first, 0.0, _norm_mod(xp_ref[...], sh, sc)[HALO - 1:HALO, :])
    next_tok = jnp.where(last, 0.0, _norm_mod(xn_ref[...], sh, sc)[0:1, :])
    row = lax.broadcasted_iota(jnp.int32, (SUBLANES, h.shape[1]), 0)
    up = jnp.where(row == 0, prev_tok, pltpu.roll(h[tm - SUBLANES:, :], 1, 0))
    down = jnp.where(row == SUBLANES - 1, next_tok, pltpu.roll(h[:SUBLANES, :], SUBLANES - 1, 0))
    zeros = jnp.zeros_like(up)
    h_scr[0:HALO, :] = jnp.concatenate([zeros, up], axis=0).astype(BF16)
    h_scr[HALO:HALO + tm, :] = h.astype(BF16)
    h_scr[HALO + tm:, :] = jnp.concatenate([down, zeros], axis=0).astype(BF16)


def _conv3(y_scr, w_ref, tm):
    return (y_scr[pl.ds(HALO - SUBLANES, tm), :] * w_ref[0:1, :]
            + y_scr[pl.ds(HALO, tm), :] * w_ref[1:2, :]
            + y_scr[pl.ds(HALO + SUBLANES, tm), :] * w_ref[2:3, :])


def _halo_specs(tm, d, m):
    r = tm // HALO
    last_blk = m // HALO - 1
    return [
        pl.BlockSpec((tm, d), lambda i, j: (i, 0)),
        pl.BlockSpec((HALO, d), lambda i, j: (jnp.maximum(i * r - 1, 0), 0)),
        pl.BlockSpec((HALO, d), lambda i, j: (jnp.minimum((i + 1) * r, last_blk), 0)),
    ]


def _conv_prologue(x_ref, xp_ref, xn_ref, sh_ref, sc_ref, h_scr, *, tm, tiles_per_seq):
    @pl.when(pl.program_id(1) == 0)
    def _():
        t = pl.program_id(0) % tiles_per_seq
        _fill_h(h_scr, x_ref, xp_ref, xn_ref, sh_ref[0], sc_ref[0], t == 0, t == tiles_per_seq - 1, tm)


def _wspec(w_layer, d, tn, off):
    layer = w_layer[1]
    return pl.BlockSpec((None, d, tn), lambda i, j: (layer, 0, j + off))


def _sconv_in_kernel(x_ref, xp_ref, xn_ref, sh_ref, sc_ref, wb_ref, wc_ref, wx_ref, cw_ref, o_ref,
                     h_scr, p_scr, *, tm, tiles_per_seq):
    _conv_prologue(x_ref, xp_ref, xn_ref, sh_ref, sc_ref, h_scr, tm=tm, tiles_per_seq=tiles_per_seq)
    h = h_scr[...]
    p_scr[...] = (jnp.dot(h, wc_ref[...], preferred_element_type=F32)
                  * jnp.dot(h, wx_ref[...], preferred_element_type=F32))
    gb = jnp.dot(h_scr[HALO:HALO + tm, :], wb_ref[...], preferred_element_type=F32)
    o_ref[...] = (gb * _conv3(p_scr, cw_ref, tm)).astype(o_ref.dtype)


def _sconv_in_call(x, shift, scale, w_layer, conv_w, *, tm, tiles_per_mod, tiles_per_seq):
    m, d = x.shape
    w = w_layer[0]
    tn = TN
    nt = w.shape[2] // 3 // tn
    mod = pl.BlockSpec((1, 1, d), lambda i, j: (i // tiles_per_mod, 0, 0))
    ext = tm + 2 * HALO
    return pl.pallas_call(
        functools.partial(_sconv_in_kernel, tm=tm, tiles_per_seq=tiles_per_seq),
        grid=(m // tm, nt),
        in_specs=_halo_specs(tm, d, m) + [
            mod, mod, _wspec(w_layer, d, tn, 0), _wspec(w_layer, d, tn, nt), _wspec(w_layer, d, tn, 2 * nt),
            pl.BlockSpec((3, tn), lambda i, j: (0, j)),
        ],
        out_specs=pl.BlockSpec((tm, tn), lambda i, j: (i, j)),
        out_shape=jax.ShapeDtypeStruct((m, nt * tn), BF16),
        scratch_shapes=[pltpu.VMEM((ext, d), BF16), pltpu.VMEM((ext, tn), F32)],
        compiler_params=_params("parallel", "arbitrary"),
        name="sconv_in",
    )(x, x, x, shift, scale, w, w, w, conv_w)


def _ffn_up_kernel(x_hbm, xp_ref, xn_ref, sh_ref, sc_ref, wg_ref, wv_ref, cg_ref, cv_ref, o_ref,
                   h_scr, yg_scr, yv_scr, x_scr, sem, *, tm, tiles_per_seq):
    i = pl.program_id(0)
    slot = i % 2

    def fetch(row, dst):
        rows = pl.ds(pl.multiple_of(row * tm, tm), tm)
        return pltpu.make_async_copy(x_hbm.at[rows, :], x_scr.at[dst], sem.at[dst])

    @pl.when(pl.program_id(1) == 0)
    def _():
        @pl.when(i == 0)
        def _():
            fetch(0, 0).start()

        fetch(i, slot).wait()

        @pl.when(i + 1 < pl.num_programs(0))
        def _():
            fetch(i + 1, 1 - slot).start()

    _conv_prologue(x_scr.at[slot], xp_ref, xn_ref, sh_ref, sc_ref, h_scr, tm=tm, tiles_per_seq=tiles_per_seq)
    h = h_scr[...]
    yg_scr[...] = jnp.dot(h, wg_ref[...], preferred_element_type=F32)
    yv_scr[...] = jnp.dot(h, wv_ref[...], preferred_element_type=F32)
    ug = _conv3(yg_scr, cg_ref, tm)
    uv = _conv3(yv_scr, cv_ref, tm)
    o_ref[...] = (ug * jax.nn.sigmoid(ug) * uv).astype(o_ref.dtype)


def _ffn_up_call(x, shift, scale, w_layer, conv_w, *, tm, tiles_per_mod, tiles_per_seq):
    m, d = x.shape
    w = w_layer[0]
    tn = TN
    nt = w.shape[2] // 2 // tn
    mod = pl.BlockSpec((1, 1, d), lambda i, j: (i // tiles_per_mod, 0, 0))
    cspec = lambda off: pl.BlockSpec((3, tn), lambda i, j: (0, j + off))
    ext = tm + 2 * HALO
    return pl.pallas_call(
        functools.partial(_ffn_up_kernel, tm=tm, tiles_per_seq=tiles_per_seq),
        grid=(m // tm, nt),
        in_specs=[pl.BlockSpec(memory_space=pl.ANY)] + _halo_specs(tm, d, m)[1:] + [
            mod, mod, _wspec(w_layer, d, tn, 0), _wspec(w_layer, d, tn, nt), cspec(0), cspec(nt)],
        out_specs=pl.BlockSpec((tm, tn), lambda i, j: (i, j)),
        out_shape=jax.ShapeDtypeStruct((m, nt * tn), BF16),
        scratch_shapes=[pltpu.VMEM((ext, d), BF16), pltpu.VMEM((ext, tn), F32), pltpu.VMEM((ext, tn), F32),
                        pltpu.VMEM((2, tm, d), F32), pltpu.SemaphoreType.DMA((2,))],
        compiler_params=_params("arbitrary", "arbitrary"),
        name="ffn_up",
    )(x, x, x, shift, scale, w, w, conv_w, conv_w)


def _proj_res_kernel(a_ref, w_ref, res_ref, g_ref, o_ref):
    y = jnp.dot(a_ref[...], w_ref[...], preferred_element_type=F32)
    o_ref[...] = res_ref[...] + g_ref[0] * y


def _proj_res_stream_kernel(a_hbm, w_ref, res_ref, g_ref, o_ref, a_scr, sem, *, tm):
    i = pl.program_id(0)
    slot = i % 2

    def fetch(row, dst):
        rows = pl.ds(pl.multiple_of(row * tm, tm), tm)
        return pltpu.make_async_copy(a_hbm.at[rows, :], a_scr.at[dst], sem.at[dst])

    @pl.when(pl.program_id(1) == 0)
    def _():
        @pl.when(i == 0)
        def _():
            fetch(0, 0).start()

        fetch(i, slot).wait()

        @pl.when(i + 1 < pl.num_programs(0))
        def _():
            fetch(i + 1, 1 - slot).start()

    y = jnp.dot(a_scr[slot], w_ref[...], preferred_element_type=F32)
    o_ref[...] = res_ref[...] + g_ref[0] * y


def _proj_res_call(a, w_layer, res, gate, *, tm, tiles_per_mod):
    m, k = a.shape
    w = w_layer[0]
    n = w.shape[2]
    tn = n if k * n * 2 <= 8 * 1024 * 1024 else TN
    if n // tn > 1:
        return pl.pallas_call(
            functools.partial(_proj_res_stream_kernel, tm=tm),
            grid=(m // tm, n // tn),
            in_specs=[
                pl.BlockSpec(memory_space=pl.ANY),
                _wspec(w_layer, k, tn, 0),
                pl.BlockSpec((tm, tn), lambda i, j: (i, j)),
                pl.BlockSpec((1, 1, tn), lambda i, j: (i // tiles_per_mod, 0, j)),
            ],
            out_specs=pl.BlockSpec((tm, tn), lambda i, j: (i, j)),
            out_shape=jax.ShapeDtypeStruct((m, n), F32),
            scratch_shapes=[pltpu.VMEM((2, tm, k), BF16), pltpu.SemaphoreType.DMA((2,))],
            compiler_params=_params("arbitrary", "arbitrary"),
            name="proj_res_stream",
        )(a, w, res, gate)
    return pl.pallas_call(
        _proj_res_kernel,
        grid=(m // tm, n // tn),
        in_specs=[
            pl.BlockSpec((tm, k), lambda i, j: (i, 0)),
            _wspec(w_layer, k, tn, 0),
            pl.BlockSpec((tm, tn), lambda i, j: (i, j)),
            pl.BlockSpec((1, 1, tn), lambda i, j: (i // tiles_per_mod, 0, j)),
        ],
        out_specs=pl.BlockSpec((tm, tn), lambda i, j: (i, j)),
        out_shape=jax.ShapeDtypeStruct((m, n), F32),
        compiler_params=_params("parallel", "arbitrary"),
        name="proj_res",
    )(a, w, res, gate)


def _final_norm_kernel(x_ref, g_ref, o_ref):
    x = x_ref[...]
    ms = jnp.mean(x * x, axis=-1, keepdims=True)
    o_ref[...] = (x * lax.rsqrt(ms + EPS)) * g_ref[...]


def _final_norm_call(x, gain, *, tm):
    m, d = x.shape
    return pl.pallas_call(
        _final_norm_kernel,
        grid=(m // tm,),
        in_specs=[pl.BlockSpec((tm, d), lambda i: (i, 0)), pl.BlockSpec((1, d), lambda i: (0, 0))],
        out_specs=pl.BlockSpec((tm, d), lambda i: (i, 0)),
        out_shape=jax.ShapeDtypeStruct((m, d), F32),
        compiler_params=_params("parallel"),
        name="final_norm",
    )(x, gain.reshape(1, d))


def _rope_tables(n_tokens):
    rows = n_tokens // GRID_W
    half = HEAD_DIM // 2
    inv_freq = 1.0 / (ROPE_THETA ** (jnp.arange(0, half, 2, dtype=F32) / half))
    ang_r = jnp.arange(rows, dtype=F32)[:, None] * inv_freq
    ang_c = jnp.arange(GRID_W, dtype=F32)[:, None] * inv_freq
    by_row = lambda a: jnp.repeat(a, GRID_W, axis=0)
    by_col = lambda a: jnp.tile(a, (rows, 1))
    cos_r, sin_r = by_row(jnp.cos(ang_r)), by_row(jnp.sin(ang_r))
    cos_c, sin_c = by_col(jnp.cos(ang_c)), by_col(jnp.sin(ang_c))
    zeros = jnp.zeros_like(cos_r)
    cos = jnp.concatenate([cos_r, cos_r, cos_c, cos_c], axis=-1)
    sin_lo = jnp.concatenate([-sin_r, zeros, -sin_c, zeros], axis=-1)
    sin_hi = jnp.concatenate([zeros, sin_r, zeros, sin_c], axis=-1)
    return cos, sin_lo, sin_hi


def kernel(x, c, ctx, c_ctx, w_ada, b_ada, attn_w_qkv, attn_w_o, attn_lambda_q1, attn_lambda_k1,
           attn_lambda_q2, attn_lambda_k2, attn_subln_gain, sconv_w_in, sconv_conv, sconv_w_out,
           ffn_w_up, ffn_conv, ffn_w_down, final_gain):
    b, t, d = x.shape
    lc = ctx.shape[1]
    depth = w_ada.shape[0]
    assert b + 1 <= ADA_ROWS and t % GRID_W == 0
    tm_l = _pick(t, (1024, 512, 256, 128))
    tm_c = _pick(lc, (256, 128))
    lat = dict(tm=tm_l, tiles_per_mod=t // tm_l)
    cx = dict(tm=tm_c, tiles_per_mod=(b * lc) // tm_c)
    lat_seq = dict(tiles_per_seq=t // tm_l, **lat)
    cx_seq = dict(tiles_per_seq=lc // tm_c, **cx)
    tm_rows = _pick(b * lc, (1024, 512, 256, 128))
    cx_rows = dict(tm=tm_rows, tiles_per_mod=(b * lc) // tm_rows)

    cond = jnp.zeros((ADA_ROWS, d), F32).at[:b].set(c).at[b].set(c_ctx)
    mods = _ada_call(cond, w_ada, b_ada).reshape(depth, ADA_ROWS, N_MOD, d)
    tables = [_to_tiles(tbl, tm_l) for tbl in _rope_tables(t)]
    ffn_up_b, ffn_down_b = ffn_w_up.astype(BF16), ffn_w_down.astype(BF16)
    attn_qkv_b, attn_o_b = attn_w_qkv.astype(BF16), attn_w_o.astype(BF16)
    sconv_in_b, sconv_out_b = sconv_w_in.astype(BF16), sconv_w_out.astype(BF16)
    last_attn = max(i for i in range(depth) if i % N_MIXERS == 0)

    xl = _to_tiles(x.reshape(b * t, d), tm_l)
    xc = _to_tiles(ctx.reshape(b * lc, d), tm_c)
    for i in range(depth):
        ctx_update = i < last_attn
        is_attn = i % N_MIXERS == 0
        ctx_read = i <= last_attn and is_attn
        ml = [mods[i, :b, k].reshape(b, 1, d) for k in range(N_MOD)]
        mc = [mods[i, b:b + 1, k].reshape(1, 1, d) for k in range(N_MOD)]
        w_up = (ffn_up_b, i)
        w_down = (ffn_down_b, i)
        if is_attn:
            a = i // N_MIXERS
            lam_init = 0.8 - 0.6 * math.exp(-0.3 * i)
            w_qkv = (attn_qkv_b, a)
            w_o = (attn_o_b, a)
            lams = [p[a].reshape(1, HEAD_DIM) for p in
                    (attn_lambda_q1, attn_lambda_k1, attn_lambda_q2, attn_lambda_k2)]
            gain = attn_subln_gain[a].reshape(1, V_HEAD_DIM)
            qt_l, k_l, vt_l = _qkv_call(xl, ml[0], ml[1], w_qkv, tables, n_seq=b, with_q=True, rope=True, **lat)
            srcs = [(k_l.reshape(b, t, QK_WIDTH), vt_l)]
            if ctx_read:
                outs_c = _qkv_call(xc, mc[0], mc[1], w_qkv, tables, n_seq=b, with_q=ctx_update, rope=False, **cx)
                k_c, vt_c = outs_c[-2].reshape(b, lc, QK_WIDTH), outs_c[-1]
                srcs.append((k_c, vt_c))
            o_l = _attn_call(qt_l, srcs, lams, gain, lam_init=lam_init)
            xl = _proj_res_call(o_l, w_o, xl, ml[2], **lat)
            if ctx_update:
                o_c = _attn_call(outs_c[0], [(k_c, vt_c)], lams, gain, lam_init=lam_init)
                xc = _proj_res_call(o_c, w_o, xc, mc[2], **cx_rows)
        else:
            jx = i // N_MIXERS
            w_in = (sconv_in_b, jx)
            w_out = (sconv_out_b, jx)
            z_l = _sconv_in_call(xl, ml[0], ml[1], w_in, sconv_conv[jx], **lat_seq)
            xl = _proj_res_call(z_l, w_out, xl, ml[2], **lat)
            if ctx_update:
                z_c = _sconv_in_call(xc, mc[0], mc[1], w_in, sconv_conv[jx], **cx_seq)
                xc = _proj_res_call(z_c, w_out, xc, mc[2], **cx_rows)
        u_l = _ffn_up_call(xl, ml[3], ml[4], w_up, ffn_conv[i], **lat_seq)
        xl = _proj_res_call(u_l, w_down, xl, ml[5], **lat)
        if ctx_update:
            u_c = _ffn_up_call(xc, mc[3], mc[4], w_up, ffn_conv[i], **cx_seq)
            xc = _proj_res_call(u_c, w_down, xc, mc[5], **cx_rows)
    return _from_tiles(_final_norm_call(xl, final_gain, tm=tm_l), tm_l).reshape(b, t, d)
```
